```python
import math
import jax, jax.numpy as jnp
from jax import lax
import numpy as np

D_MODEL = 1024
BATCH = 2
SEQ = 16384
DEPTH = 2
DEC_BATCH = 32
DEC_SEQ = 64
PAST_LEN = 4096

CHUNK = 64
Q_BLOCK = 128
N_A_LAYERS = max(1, DEPTH // 2)
N_B_LAYERS = DEPTH - N_A_LAYERS
HGRN_EXPAND = 128
H_A = D_MODEL // HGRN_EXPAND
DK_A = HGRN_EXPAND
DV_A = D_MODEL // H_A
D_A = H_A * DK_A
H_B = 8
DH_B = D_MODEL // (2 * H_B)
N_KV_B = 4
G_B = H_B // N_KV_B
ROT_DIM = DH_B // 4
ROPE_THETA = 500000.0
LAMBDA_INIT_SCALE = 0.3
D_FF = 4 * D_MODEL
EPS = 1e-6

kernel_name = "yoco_hgrn2_diffattn_stream_step"


def rms_norm(x, gain):
    xf = x.astype(jnp.float32)
    y = xf * lax.rsqrt(jnp.mean(xf * xf, axis=-1, keepdims=True) + EPS)
    return (y * gain.astype(jnp.float32)).astype(x.dtype)


def squared_relu_mlp(h, w_up, w_down):
    u = jax.nn.relu(h @ w_up)
    return (u * u) @ w_down


def hgrn2_chunk_step(S, blk):
    q, k, v, g = blk
    C = q.shape[1]
    b = jnp.cumsum(g, axis=1)
    causal = jnp.tril(jnp.ones((C, C), dtype=bool))
    diff = b[:, :, None] - b[:, None, :]
    decay = jnp.where(causal[None, :, :, None, None], jnp.exp(jnp.minimum(diff, 0.0)), 0.0)
    scores = jnp.einsum('bthd,btshd,bshd->bhts', q, decay, k)
    o = (jnp.einsum('bhts,bshv->bthv', scores, v)
         + jnp.einsum('bthd,bhdv->bthv', q * jnp.exp(b), S))
    b_last = b[:, -1]
    S = (jnp.exp(b_last)[..., None] * S
         + jnp.einsum('bshd,bshv->bhdv', k * jnp.exp(b_last[:, None] - b), v))
    return S, o


def hgrn2_mixer(h, S0, w_in, lb, g_gain, w_out):
    B, L, _ = h.shape
    q, f, i, gate = jnp.split(h @ w_in, 4, axis=-1)
    forget = lb + (1.0 - lb) * jax.nn.sigmoid(f.astype(jnp.float32))
    heads = lambda t, d: t.reshape(B, L, H_A, d)
    q = heads(jax.nn.silu(q.astype(jnp.float32)) * DK_A ** -0.5, DK_A)
    k = heads(1.0 - forget, DK_A)
    g = heads(jnp.log(forget), DK_A)
    v = heads(i.astype(jnp.float32), DV_A)
    C = min(CHUNK, L)
    n = L // C
    to_blocks = lambda t: t.reshape(B, n, C, H_A, t.shape[-1]).transpose(1, 0, 2, 3, 4)
    S, o = lax.scan(hgrn2_chunk_step, S0.astype(jnp.float32),
                    (to_blocks(q), to_blocks(k), to_blocks(v), to_blocks(g)))
    o = o.transpose(1, 0, 2, 3, 4).reshape(B, L, H_A, DV_A)
    o = rms_norm(o, g_gain) * jax.nn.silu(heads(gate, DV_A).astype(jnp.float32))
    return o.reshape(B, L, D_A).astype(h.dtype) @ w_out, S


def rope_partial(x, pos):
    inv_freq = ROPE_THETA ** (-jnp.arange(0, ROT_DIM, 2, dtype=jnp.float32) / ROT_DIM)
    ang = pos.astype(jnp.float32)[:, None] * inv_freq[None, :]
    cos = jnp.cos(ang)[None, :, None, None, :]
    sin = jnp.sin(ang)[None, :, None, None, :]
    xr = x[..., :ROT_DIM].astype(jnp.float32)
    x1, x2 = xr[..., :ROT_DIM // 2], xr[..., ROT_DIM // 2:]
    rot = jnp.concatenate([x1 * cos - x2 * sin, x2 * cos + x1 * sin], axis=-1).astype(x.dtype)
    return jnp.concatenate([rot, x[..., ROT_DIM:]], axis=-1)


def shared_kv(h, norm_kv, w_kv, pos):
    B, L, _ = h.shape
    k, v = jnp.split(rms_norm(h, norm_kv) @ w_kv, 2, axis=-1)
    k = rope_partial(k.reshape(B, L, N_KV_B, 2, DH_B), pos).reshape(B, L, N_KV_B, 2 * DH_B)
    v = v.reshape(B, L, N_KV_B, 2 * DH_B)
    return k, v


def diff_attention(h, k_all, v_all, q_pos, k_pos, w_q, lq1, lk1, lq2, lk2, sub_gain, w_out, lam_init):
    B, L, _ = h.shape
    Lk = k_all.shape[1]
    q = rope_partial((h @ w_q).reshape(B, L, H_B, 2, DH_B), q_pos)
    lam = (jnp.exp(jnp.sum(lq1.astype(jnp.float32) * lk1.astype(jnp.float32)))
           - jnp.exp(jnp.sum(lq2.astype(jnp.float32) * lk2.astype(jnp.float32))) + lam_init)
    k = k_all.reshape(B, Lk, N_KV_B, 2, DH_B)
    v = v_all.astype(jnp.float32)
    k_chunk = k_pos // CHUNK
    QB = min(Q_BLOCK, L)
    nb = L // QB
    q_blocks = q.reshape(B, nb, QB, N_KV_B, G_B, 2, DH_B).transpose(1, 0, 2, 3, 4, 5, 6)
    qc_blocks = (q_pos // CHUNK).reshape(nb, QB)

    def block(args):
        qb, qc = args
        s = jnp.einsum('bqkgmd,bskmd->bkgmqs', qb, k,
                       preferred_element_type=jnp.float32) * DH_B ** -0.5
        mask = k_chunk[None, :] <= qc[:, None]
        p = jax.nn.softmax(jnp.where(mask, s, -jnp.inf), axis=-1)
        a = p[:, :, :, 0] - lam * p[:, :, :, 1]
        return jnp.einsum('bkgqs,bskv->bqkgv', a, v)

    o = lax.map(block, (q_blocks, qc_blocks))
    o = o.transpose(1, 0, 2, 3, 4, 5).reshape(B, L, H_B, 2 * DH_B)
    o = rms_norm(o, sub_gain) * (1.0 - lam_init)
    return o.reshape(B, L, H_B * 2 * DH_B).astype(h.dtype) @ w_out


def setup_inputs(seed: int = 0) -> dict:
    key = jax.random.key(seed)
    ks = iter(jax.random.split(key, 32))
    nrm = lambda shape, scale: jax.random.normal(next(ks), shape, jnp.float32) * scale
    gain = lambda shape: 1.0 + nrm(shape, 0.02)
    return {
        "x_prompt": nrm((BATCH, SEQ, D_MODEL), 1.0),
        "x_sample": nrm((DEC_BATCH, DEC_SEQ, D_MODEL), 1.0),
        "state_hgrn": nrm((N_A_LAYERS, DEC_BATCH, H_A, DK_A, DV_A), 0.5),
        "cache_k": nrm((DEC_BATCH, PAST_LEN, N_KV_B, 2 * DH_B), 1.0),
        "cache_v": nrm((DEC_BATCH, PAST_LEN, N_KV_B, 2 * DH_B), 1.0),
        "norm_mix_pre": gain((DEPTH, D_MODEL)),
        "norm_mix_post": gain((DEPTH, D_MODEL)),
        "norm_mlp_pre": gain((DEPTH, D_MODEL)),
        "norm_mlp_post": gain((DEPTH, D_MODEL)),
        "w_up": nrm((DEPTH, D_MODEL, D_FF), D_MODEL ** -0.5),
        "w_down": nrm((DEPTH, D_FF, D_MODEL), D_FF ** -0.5),
        "w_in_a": nrm((N_A_LAYERS, D_MODEL, 4 * D_A), D_MODEL ** -0.5),
        "lb_logits": nrm((DEPTH, D_A), 0.5),
        "gnorm_a": gain((N_A_LAYERS, DV_A)),
        "w_out_a": nrm((N_A_LAYERS, D_A, D_MODEL), D_A ** -0.5),
        "norm_kv": gain((D_MODEL,)),
        "w_kv": nrm((D_MODEL, 2 * N_KV_B * 2 * DH_B), D_MODEL ** -0.5),
        "w_q_b": nrm((N_B_LAYERS, D_MODEL, H_B * 2 * DH_B), D_MODEL ** -0.5),
        "lam_q1": nrm((N_B_LAYERS, DH_B), 0.1),
        "lam_k1": nrm((N_B_LAYERS, DH_B), 0.1),
        "lam_q2": nrm((N_B_LAYERS, DH_B), 0.1),
        "lam_k2": nrm((N_B_LAYERS, DH_B), 0.1),
        "subln_b": gain((N_B_LAYERS, 2 * DH_B)),
        "w_out_b": nrm((N_B_LAYERS, H_B * 2 * DH_B, D_MODEL), (H_B * 2 * DH_B) ** -0.5),
    }


def reference(x_prompt, x_sample, state_hgrn, cache_k, cache_v,
              norm_mix_pre, norm_mix_post, norm_mlp_pre, norm_mlp_post, w_up, w_down,
              w_in_a, lb_logits, gnorm_a, w_out_a, norm_kv, w_kv,
              w_q_b, lam_q1, lam_k1, lam_q2, lam_k2, subln_b, w_out_b):
    lb_all = jnp.cumsum(jax.nn.softmax(lb_logits.astype(jnp.float32), axis=0), axis=0)

    def trunk(x, S0, past_k, past_v):
        B, L, _ = x.shape
        pos0 = 0 if past_k is None else past_k.shape[1]
        pos = pos0 + jnp.arange(L, dtype=jnp.int32)
        h = x
        states = []
        k_new = v_new = k_all = v_all = k_pos = None
        for l in range(DEPTH):
            u = rms_norm(h, norm_mix_pre[l])
            if l < N_A_LAYERS:
                mix, S = hgrn2_mixer(u, S0[l], w_in_a[l], lb_all[l], gnorm_a[l], w_out_a[l])
                states.append(S)
            else:
                j = l - N_A_LAYERS
                if j == 0:
                    k_new, v_new = shared_kv(h, norm_kv, w_kv, pos)
                    if past_k is None:
                        k_all, v_all = k_new, v_new
                    else:
                        k_all = jnp.concatenate([past_k.astype(k_new.dtype), k_new], axis=1)
                        v_all = jnp.concatenate([past_v.astype(v_new.dtype), v_new], axis=1)
                    k_pos = jnp.arange(k_all.shape[1], dtype=jnp.int32)
                lam_init = 0.8 - 0.6 * math.exp(-LAMBDA_INIT_SCALE * l)
                mix = diff_attention(u, k_all, v_all, pos, k_pos, w_q_b[j], lam_q1[j], lam_k1[j],
                                     lam_q2[j], lam_k2[j], subln_b[j], w_out_b[j], lam_init)
            h = h + rms_norm(mix, norm_mix_post[l])
            h = h + rms_norm(squared_relu_mlp(rms_norm(h, norm_mlp_pre[l]), w_up[l], w_down[l]),
                             norm_mlp_post[l])
        return h, jnp.stack(states), k_new, v_new

    S0_prompt = jnp.zeros((N_A_LAYERS, x_prompt.shape[0], H_A, DK_A, DV_A), jnp.float32)
    y_prompt, state_hgrn_prompt, k_prompt, v_prompt = trunk(x_prompt, S0_prompt, None, None)
    y_sample, state_hgrn_sample, k_sample, v_sample = trunk(x_sample, state_hgrn, cache_k, cache_v)
    return (y_prompt, y_sample, state_hgrn_prompt, k_prompt, v_prompt,
            state_hgrn_sample, k_sample, v_sample)
```

```python
import functools
import math

import jax
import jax.numpy as jnp
from jax import lax
from jax.experimental import pallas as pl
from jax.experimental.pallas import tpu as pltpu

F32 = jnp.float32
BF16 = jnp.bfloat16

D_MODEL = 1024
CHUNK = 64
SUB = 16
N_SUB = CHUNK // SUB
H_A = 8
DK_A = 128
DV_A = 128
D_A = H_A * DK_A
H_B = 8
DH_B = 64
N_KV_B = 4
G_B = H_B // N_KV_B
HEAD_W = 2 * DH_B
ROT_DIM = DH_B // 4
ROT_HALF = ROT_DIM // 2
ROPE_THETA = 500000.0
LAMBDA_INIT_SCALE = 0.3
D_FF = 4 * D_MODEL
FF_BLOCK = 1024
EPS = 1e-6
NEG_BIG = -1e30
LANE = 128

VMEM_LIMIT = 52 * 1024 * 1024


def _rms(x):
    return x * lax.rsqrt(jnp.mean(x * x, axis=-1, keepdims=True) + EPS)


def _sigmoid(x):
    return 1.0 / (1.0 + jnp.exp(-x))


def _dot(a, b):
    return jnp.dot(a, b, preferred_element_type=F32)


def _dot_nt(a, b):
    return lax.dot_general(a, b, (((1,), (1,)), ((), ())), preferred_element_type=F32)


def _const_spec(shape):
    nd = len(shape)
    return pl.BlockSpec(shape, lambda *_: (0,) * nd)


def _cumsum_rows(tri, g):
    hi = g.astype(BF16)
    r1 = g - hi.astype(F32)
    mid = r1.astype(BF16)
    lo = (r1 - mid.astype(F32)).astype(BF16)
    return _dot(tri, hi) + _dot(tri, mid) + _dot(tri, lo)


def _hgrn_head_chunk(q, k, g, v, st):
    rows = lax.broadcasted_iota(jnp.int32, (CHUNK, CHUNK), 0)
    cols = lax.broadcasted_iota(jnp.int32, (CHUNK, CHUNK), 1)
    tri = (rows >= cols).astype(BF16)
    b = _cumsum_rows(tri, g)

    qe = q * jnp.exp(b)
    o = _dot_nt(qe.astype(BF16), st.astype(BF16))

    refs = [b[SUB * i - 1:SUB * i, :] for i in range(1, N_SUB)]
    ref_rows = jnp.concatenate(
        [jnp.zeros((SUB, DK_A), F32)]
        + [jnp.broadcast_to(r, (SUB, DK_A)) for r in refs], axis=0)
    qd = q * jnp.exp(b - ref_rows)
    kd = jnp.concatenate(
        [k[:SUB * i] * jnp.exp(refs[i - 1] - b[:SUB * i]) for i in range(1, N_SUB)], axis=0)
    v_all = jnp.concatenate([v[:SUB * i] for i in range(1, N_SUB)], axis=0)
    n_off = kd.shape[0]
    sc = _dot_nt(qd.astype(BF16), kd.astype(BF16))
    r_blk = lax.broadcasted_iota(jnp.int32, (CHUNK, n_off), 0) // SUB
    c_idx = lax.broadcasted_iota(jnp.int32, (CHUNK, n_off), 1)
    c_grp = jnp.zeros((CHUNK, n_off), jnp.int32)
    start = 0
    for i in range(1, N_SUB):
        c_grp = jnp.where(c_idx >= start, i, c_grp)
        start += SUB * i
    sc = jnp.where(r_blk == c_grp, sc, 0.0)
    o = o + _dot(sc.astype(BF16), v_all.astype(BF16))

    t_idx = lax.broadcasted_iota(jnp.int32, (SUB, 1), 0)
    diag = []
    for i in range(N_SUB):
        sl = slice(SUB * i, SUB * (i + 1))
        b_r, q_r, k_r, v_r = b[sl], q[sl], k[sl], v[sl]
        acc = jnp.zeros((SUB, DV_A), F32)
        for s in range(SUB):
            dec = jnp.exp(jnp.minimum(b_r - b_r[s:s + 1, :], 0.0))
            col = jnp.sum(q_r * dec * k_r[s:s + 1, :], axis=-1, keepdims=True)
            col = jnp.where(t_idx >= s, col, 0.0)
            acc = acc + col * v_r[s:s + 1, :]
        diag.append(acc)
    o = o + jnp.concatenate(diag, axis=0)

    b_last = b[CHUNK - 1:CHUNK, :]
    kdl = k * jnp.exp(b_last - b)
    st_new = st * jnp.exp(b_last) + _dot(v.T.astype(BF16), kdl.astype(BF16))
    return o, st_new


def _hgrn_kernel(x_ref, s0_ref, gpre_ref, win_ref, lbl_ref, gg_ref,
                 o_ref, sout_ref,
                 q_s, k_s, g_s, v_s, gate_s, o_s, st_s, *, tile):
    t = pl.program_id(1)

    @pl.when(t == 0)
    def _():
        for h in range(H_A):
            st_s[h] = s0_ref[0, h].T

    lbl = lbl_ref[...]
    e = jnp.exp(lbl - jnp.max(lbl, axis=0, keepdims=True))
    lb = e[0:1, :] / jnp.sum(e, axis=0, keepdims=True)

    u = (_rms(x_ref[0]) * gpre_ref[...]).astype(BF16)
    qf = _dot(u, win_ref[:, 0 * D_A:1 * D_A])
    ff = _dot(u, win_ref[:, 1 * D_A:2 * D_A])
    vf = _dot(u, win_ref[:, 2 * D_A:3 * D_A])
    gf = _dot(u, win_ref[:, 3 * D_A:4 * D_A])
    forget = lb + (1.0 - lb) * _sigmoid(ff)
    q = qf * _sigmoid(qf) * (DK_A ** -0.5)
    k = 1.0 - forget
    g = jnp.log(forget)
    gate = gf * _sigmoid(gf)
    for h in range(H_A):
        sl = slice(h * DK_A, (h + 1) * DK_A)
        q_s[h] = q[:, sl]
        k_s[h] = k[:, sl]
        g_s[h] = g[:, sl]
        v_s[h] = vf[:, sl]
        gate_s[h] = gate[:, sl]

    gg = gg_ref[...]

    def chunk_body(c, carry):
        r0 = pl.multiple_of(c * CHUNK, CHUNK)
        rs = pl.ds(r0, CHUNK)

        def head_body(h, carry2):
            o, st_new = _hgrn_head_chunk(q_s[h, rs, :], k_s[h, rs, :], g_s[h, rs, :],
                                         v_s[h, rs, :], st_s[h])
            st_s[h] = st_new
            o_s[h, rs, :] = _rms(o) * gg * gate_s[h, rs, :]
            return carry2

        return lax.fori_loop(0, H_A, head_body, carry)

    lax.fori_loop(0, tile // CHUNK, chunk_body, 0)

    for h in range(H_A):
        o_ref[0, :, h * DV_A:(h + 1) * DV_A] = o_s[h].astype(BF16)

    @pl.when(t == pl.num_programs(1) - 1)
    def _():
        for h in range(H_A):
            sout_ref[0, h] = st_s[h].T


def _hgrn_call(x, s0, gpre, w_in, lb_logits, ggain, *, tile):
    B, L, _ = x.shape
    grid = (B, L // tile)
    head_scratch = pltpu.VMEM((H_A, tile, DK_A), F32)
    return pl.pallas_call(
        functools.partial(_hgrn_kernel, tile=tile),
        grid=grid,
        in_specs=[
            pl.BlockSpec((1, tile, D_MODEL), lambda b, t: (b, t, 0)),
            pl.BlockSpec((1, H_A, DK_A, DV_A), lambda b, t: (b, 0, 0, 0)),
            _const_spec((1, D_MODEL)),
            _const_spec((D_MODEL, 4 * D_A)),
            _const_spec(lb_logits.shape),
            _const_spec((1, DV_A)),
        ],
        out_specs=[
            pl.BlockSpec((1, tile, D_A), lambda b, t: (b, t, 0)),
            pl.BlockSpec((1, H_A, DK_A, DV_A), lambda b, t: (b, 0, 0, 0)),
        ],
        out_shape=[
            jax.ShapeDtypeStruct((B, L, D_A), BF16),
            jax.ShapeDtypeStruct((B, H_A, DK_A, DV_A), F32),
        ],
        scratch_shapes=[head_scratch] * 6 + [pltpu.VMEM((H_A, DV_A, DK_A), F32)],
        compiler_params=pltpu.CompilerParams(
            dimension_semantics=("arbitrary", "arbitrary"),
            vmem_limit_bytes=VMEM_LIMIT),
        name="hgrn",
    )(x, s0, gpre, w_in, lb_logits, ggain)


def _mlp_kernel(x_ref, a_ref, wo_ref, gpost_ref, gpre_ref, wup_ref, wdn_ref, gmpost_ref, y_ref):
    h = x_ref[...] + _rms(_dot(a_ref[...], wo_ref[...])) * gpost_ref[...]
    hn = (_rms(h) * gpre_ref[...]).astype(BF16)
    acc = jnp.zeros(h.shape, F32)
    for c in range(D_FF // FF_BLOCK):
        sl = slice(c * FF_BLOCK, (c + 1) * FF_BLOCK)
        u = jnp.maximum(_dot(hn, wup_ref[:, sl]), 0.0)
        acc = acc + _dot((u * u).astype(BF16), wdn_ref[sl, :])
    y_ref[...] = h + _rms(acc) * gmpost_ref[...]


def _mlp_call(x, a, w_o, gpost, gpre, w_up, w_down, gmpost, *, tile):
    n = x.shape[0]
    row = lambda i: (i, 0)
    return pl.pallas_call(
        _mlp_kernel,
        grid=(n // tile,),
        in_specs=[
            pl.BlockSpec((tile, D_MODEL), row),
            pl.BlockSpec((tile, D_MODEL), row),
            _const_spec((D_MODEL, D_MODEL)),
            _const_spec((1, D_MODEL)),
            _const_spec((1, D_MODEL)),
            _const_spec((D_MODEL, D_FF)),
            _const_spec((D_FF, D_MODEL)),
            _const_spec((1, D_MODEL)),
        ],
        out_specs=pl.BlockSpec((tile, D_MODEL), row),
        out_shape=jax.ShapeDtypeStruct((n, D_MODEL), F32),
        compiler_params=pltpu.CompilerParams(
            dimension_semantics=("arbitrary",),
            vmem_limit_bytes=VMEM_LIMIT),
        name="mlp",
    )(x, a, w_o, gpost, gpre, w_up, w_down, gmpost)


def _rope_tables(pos):
    inv_freq = ROPE_THETA ** (-jnp.arange(0, ROT_DIM, 2, dtype=F32) / ROT_DIM)
    ang = pos.astype(F32)[:, None] * inv_freq[None, :]
    cos, sin = jnp.cos(ang), jnp.sin(ang)
    n = pos.shape[0]
    rest = DH_B - ROT_DIM
    one = jnp.ones((n, rest), F32)
    zero = jnp.zeros((n, rest), F32)
    zh = jnp.zeros((n, ROT_HALF), F32)
    c64 = jnp.concatenate([cos, cos, one], axis=1)
    up64 = jnp.concatenate([-sin, zh, zero], axis=1)
    dn64 = jnp.concatenate([zh, sin, zero], axis=1)
    twice = lambda t: jnp.concatenate([t, t], axis=1)
    return twice(c64), twice(up64), twice(dn64)


def _qkv_kernel(h_ref, gkv_ref, gq_ref, wkv_ref, wq_ref, cos_ref, up_ref, dn_ref,
                k_ref, v_ref, q_ref):
    xh = _rms(h_ref[...])
    kv = _dot((xh * gkv_ref[...]).astype(BF16), wkv_ref[...])
    q = _dot((xh * gq_ref[...]).astype(BF16), wq_ref[...])
    cos, up, dn = cos_ref[...], up_ref[...], dn_ref[...]

    def rope(x):
        return (x * cos + pltpu.roll(x, LANE - ROT_HALF, axis=1) * up
                + pltpu.roll(x, ROT_HALF, axis=1) * dn)

    n_k = N_KV_B * HEAD_W
    for j in range(N_KV_B):
        sl = slice(j * HEAD_W, (j + 1) * HEAD_W)
        k_ref[:, sl] = rope(kv[:, sl])
    v_ref[...] = kv[:, n_k:]
    for j in range(H_B):
        sl = slice(j * HEAD_W, (j + 1) * HEAD_W)
        q_ref[:, sl] = (rope(q[:, sl]) * (DH_B ** -0.5)).astype(BF16)


def _qkv_call(h, gkv, gq, w_kv, w_q, tables, *, tile):
    n = h.shape[0]
    n_tab = tables[0].shape[0] // tile
    row = lambda i: (i, 0)
    tab = pl.BlockSpec((tile, LANE), lambda i: (i % n_tab, 0))
    n_k = N_KV_B * HEAD_W
    return pl.pallas_call(
        _qkv_kernel,
        grid=(n // tile,),
        in_specs=[
            pl.BlockSpec((tile, D_MODEL), row),
            _const_spec((1, D_MODEL)),
            _const_spec((1, D_MODEL)),
            _const_spec((D_MODEL, 2 * n_k)),
            _const_spec((D_MODEL, H_B * HEAD_W)),
            tab, tab, tab,
        ],
        out_specs=[
            pl.BlockSpec((tile, n_k), row),
            pl.BlockSpec((tile, n_k), row),
            pl.BlockSpec((tile, H_B * HEAD_W), row),
        ],
        out_shape=[
            jax.ShapeDtypeStruct((n, n_k), F32),
            jax.ShapeDtypeStruct((n, n_k), F32),
            jax.ShapeDtypeStruct((n, H_B * HEAD_W), BF16),
        ],
        compiler_params=pltpu.CompilerParams(
            dimension_semantics=("arbitrary",),
            vmem_limit_bytes=VMEM_LIMIT),
        name="qkv",
    )(h, gkv, gq, w_kv, w_q, *tables)


N_STACK = 2 * G_B


def _attn_init(q_ref, qs, m_s, l_s, acc_s, tq):
    q = q_ref[0]
    lane = lax.broadcasted_iota(jnp.int32, (tq, HEAD_W), 1)
    for g in range(G_B):
        qg = q[:, g * HEAD_W:(g + 1) * HEAD_W]
        qs[(2 * g) * tq:(2 * g + 1) * tq, :] = jnp.where(lane < DH_B, qg, jnp.zeros_like(qg))
        qs[(2 * g + 1) * tq:(2 * g + 2) * tq, :] = jnp.where(lane >= DH_B, qg, jnp.zeros_like(qg))
    m_s[...] = jnp.full(m_s.shape, NEG_BIG, F32)
    l_s[...] = jnp.zeros(l_s.shape, F32)
    acc_s[...] = jnp.zeros(acc_s.shape, F32)


def _attn_step(k, v, qs, m_s, l_s, acc_s, mask):
    s = _dot_nt(qs[...], k)
    if mask is not None:
        s = jnp.where(mask, s, NEG_BIG)
    m_prev = m_s[...]
    m_new = jnp.maximum(m_prev, jnp.max(s, axis=1, keepdims=True))
    alpha = jnp.exp(m_prev - m_new)
    p = jnp.exp(s - m_new)
    l_s[...] = alpha * l_s[...] + jnp.sum(p, axis=1, keepdims=True)
    acc_s[...] = alpha * acc_s[...] + _dot(p.astype(BF16), v)
    m_s[...] = m_new


def _chunk_mask(q_pos0, k_pos0, tq, tk):
    r = lax.broadcasted_iota(jnp.int32, (N_STACK * tq, 1), 0)
    qc = (q_pos0 + r % tq) // CHUNK
    kc = (k_pos0 + lax.broadcasted_iota(jnp.int32, (1, tk), 1)) // CHUNK
    return kc <= qc


def _attn_finish(o_ref, lq1, lk1, lq2, lk2, sub_ref, l_s, acc_s, tq, lam_init):
    lam = (jnp.exp(jnp.sum(lq1[...] * lk1[...], axis=1, keepdims=True))
           - jnp.exp(jnp.sum(lq2[...] * lk2[...], axis=1, keepdims=True)) + lam_init)
    o = acc_s[...] / l_s[...]
    for g in range(G_B):
        og = o[(2 * g) * tq:(2 * g + 1) * tq] - lam * o[(2 * g + 1) * tq:(2 * g + 2) * tq]
        og = _rms(og) * sub_ref[...] * (1.0 - lam_init)
        o_ref[0, :, g * HEAD_W:(g + 1) * HEAD_W] = og.astype(BF16)


def _attn_prompt_kernel(q_ref, k_ref, v_ref, lq1, lk1, lq2, lk2, sub_ref, o_ref,
                        qs, m_s, l_s, acc_s, *, tq, tk, lam_init):
    i, j = pl.program_id(2), pl.program_id(3)

    @pl.when(j == 0)
    def _():
        _attn_init(q_ref, qs, m_s, l_s, acc_s, tq)

    n_full = (i * tq + CHUNK) // tk
    n_need = ((i + 1) * tq + tk - 1) // tk

    @pl.when(j < n_full)
    def _():
        _attn_step(k_ref[0].astype(BF16), v_ref[0].astype(BF16), qs, m_s, l_s, acc_s, None)

    @pl.when(jnp.logical_and(j >= n_full, j < n_need))
    def _():
        mask = _chunk_mask(i * tq, j * tk, tq, tk)
        _attn_step(k_ref[0].astype(BF16), v_ref[0].astype(BF16), qs, m_s, l_s, acc_s, mask)

    @pl.when(j == pl.num_programs(3) - 1)
    def _():
        _attn_finish(o_ref, lq1, lk1, lq2, lk2, sub_ref, l_s, acc_s, tq, lam_init)


def _attn_scratch(tq):
    return [
        pltpu.VMEM((N_STACK * tq, HEAD_W), BF16),
        pltpu.VMEM((N_STACK * tq, 1), F32),
        pltpu.VMEM((N_STACK * tq, 1), F32),
        pltpu.VMEM((N_STACK * tq, HEAD_W), F32),
    ]


def _attn_prompt_call(q, k, v, lam_params, sub_gain, *, tq, tk, lam_init):
    B, L, _ = q.shape
    nq, nk = L // tq, L // tk

    def kv_map(b, h, i, j):
        last = ((i + 1) * tq + tk - 1) // tk - 1
        return (b, jnp.minimum(j, last), h)

    kv_spec = pl.BlockSpec((1, tk, HEAD_W), kv_map)
    q_spec = pl.BlockSpec((1, tq, G_B * HEAD_W), lambda b, h, i, j: (b, i, h))
    small = [_const_spec((1, DH_B))] * 4 + [_const_spec((1, HEAD_W))]
    return pl.pallas_call(
        functools.partial(_attn_prompt_kernel, tq=tq, tk=tk, lam_init=lam_init),
        grid=(B, N_KV_B, nq, nk),
        in_specs=[q_spec, kv_spec, kv_spec] + small,
        out_specs=q_spec,
        out_shape=jax.ShapeDtypeStruct((B, L, H_B * HEAD_W), BF16),
        scratch_shapes=_attn_scratch(tq),
        compiler_params=pltpu.CompilerParams(
            dimension_semantics=("arbitrary",) * 4,
            vmem_limit_bytes=VMEM_LIMIT),
        name="attn_prompt",
    )(q, k, v, *lam_params, sub_gain)


def _attn_sample_kernel(q_ref, pk_ref, pv_ref, nk_ref, nv_ref, lq1, lk1, lq2, lk2, sub_ref, o_ref,
                        qs, m_s, l_s, acc_s, *, tq, tk, n_past, past_len, lam_init):
    j = pl.program_id(2)

    @pl.when(j == 0)
    def _():
        _attn_init(q_ref, qs, m_s, l_s, acc_s, tq)

    @pl.when(j < n_past)
    def _():
        _attn_step(pk_ref[0].astype(BF16), pv_ref[0].astype(BF16), qs, m_s, l_s, acc_s, None)

    @pl.when(j == n_past)
    def _():
        mask = _chunk_mask(past_len, past_len, tq, tq)
        _attn_step(nk_ref[0].astype(BF16), nv_ref[0].astype(BF16), qs, m_s, l_s, acc_s, mask)
        _attn_finish(o_ref, lq1, lk1, lq2, lk2, sub_ref, l_s, acc_s, tq, lam_init)


def _attn_sample_call(q, past_k, past_v, k_new, v_new, lam_params, sub_gain, *, tk, lam_init):
    B, tq, _ = q.shape
    past_len = past_k.shape[1]
    assert past_len % tk == 0 and past_len % CHUNK == 0
    n_past = past_len // tk
    past_spec = pl.BlockSpec((1, tk, HEAD_W), lambda b, h, j: (b, jnp.minimum(j, n_past - 1), h))
    new_spec = pl.BlockSpec((1, tq, HEAD_W), lambda b, h, j: (b, 0, h))
    q_spec = pl.BlockSpec((1, tq, G_B * HEAD_W), lambda b, h, j: (b, 0, h))
    small = [_const_spec((1, DH_B))] * 4 + [_const_spec((1, HEAD_W))]
    return pl.pallas_call(
        functools.partial(_attn_sample_kernel, tq=tq, tk=tk, n_past=n_past,
                          past_len=past_len, lam_init=lam_init),
        grid=(B, N_KV_B, n_past + 1),
        in_specs=[q_spec, past_spec, past_spec, new_spec, new_spec] + small,
        out_specs=q_spec,
        out_shape=jax.ShapeDtypeStruct((B, tq, H_B * HEAD_W), BF16),
        scratch_shapes=_attn_scratch(tq),
        compiler_params=pltpu.CompilerParams(
            dimension_semantics=("arbitrary",) * 3,
            vmem_limit_bytes=VMEM_LIMIT),
        name="attn_sample",
    )(q, past_k, past_v, k_new, v_new, *lam_params, sub_gain)


def _pick_tile(n, want):
    t = min(n, want)
    assert n % t == 0
    return t


def kernel(x_prompt, x_sample, state_hgrn, cache_k, cache_v, norm_mix_pre, norm_mix_post,
           norm_mlp_pre, norm_mlp_post, w_up, w_down, w_in_a, lb_logits, gnorm_a, w_out_a,
           norm_kv, w_kv, w_q_b, lam_q1, lam_k1, lam_q2, lam_k2, subln_b, w_out_b):
    row = lambda a: a.reshape(1, -1)
    w_in_bf = w_in_a[0].astype(BF16)
    w_out_a_bf = w_out_a[0].astype(BF16)
    w_up_bf = w_up.astype(BF16)
    w_down_bf = w_down.astype(BF16)
    w_kv_bf = w_kv.astype(BF16)
    w_q_bf = w_q_b[0].astype(BF16)
    w_out_b_bf = w_out_b[0].astype(BF16)
    lam_params = [row(lam_q1[0]), row(lam_k1[0]), row(lam_q2[0]), row(lam_k2[0])]
    lam_init = 0.8 - 0.6 * math.exp(-LAMBDA_INIT_SCALE * 1)

    def mlp(layer, x, a, w_o):
        return _mlp_call(x, a, w_o, row(norm_mix_post[layer]), row(norm_mlp_pre[layer]),
                         w_up_bf[layer], w_down_bf[layer], row(norm_mlp_post[layer]),
                         tile=_pick_tile(x.shape[0], 512))

    def trunk(x, s0, past_k, past_v):
        B, L, _ = x.shape
        n = B * L
        o_a, s_out = _hgrn_call(x, s0, row(norm_mix_pre[0]), w_in_bf, lb_logits, row(gnorm_a[0]),
                                tile=_pick_tile(L, 256))
        h = mlp(0, x.reshape(n, D_MODEL), o_a.reshape(n, D_A), w_out_a_bf)

        pos0 = 0 if past_k is None else past_k.shape[1]
        tile = _pick_tile(n, 512)
        reps = max(1, tile // L)
        tables = _rope_tables(pos0 + jnp.arange(L, dtype=jnp.int32))
        tables = [jnp.tile(t, (reps, 1)) for t in tables]
        k, v, q = _qkv_call(h, row(norm_kv), row(norm_mix_pre[1]), w_kv_bf, w_q_bf, tables, tile=tile)

        n_k = N_KV_B * HEAD_W
        q3, k3, v3 = q.reshape(B, L, -1), k.reshape(B, L, n_k), v.reshape(B, L, n_k)
        if past_k is None:
            o_b = _attn_prompt_call(q3, k3, v3, lam_params, row(subln_b[0]),
                                    tq=256, tk=512, lam_init=lam_init)
        else:
            P = past_k.shape[1]
            o_b = _attn_sample_call(q3, past_k.reshape(B, P, n_k), past_v.reshape(B, P, n_k),
                                    k3, v3, lam_params, row(subln_b[0]), tk=512, lam_init=lam_init)
        y = mlp(1, h, o_b.reshape(n, D_MODEL), w_out_b_bf)
        return (y.reshape(B, L, D_MODEL), s_out[None],
                k.reshape(B, L, N_KV_B, HEAD_W), v.reshape(B, L, N_KV_B, HEAD_W))

    s0_prompt = jnp.zeros((x_prompt.shape[0], H_A, DK_A, DV_A), F32)
    y_p, s_p, k_p, v_p = trunk(x_prompt, s0_prompt, None, None)
    y_s, s_s, k_s, v_s = trunk(x_sample, state_hgrn[0], cache_k, cache_v)
    return (y_p, y_s, s_p, k_p, v_p, s_s, k_s, v_s)
```

```python
import functools
import math

import jax
import jax.numpy as jnp
from jax import lax
from jax.experimental import pallas as pl
from jax.experimental.pallas import tpu as pltpu

F32 = jnp.float32
BF16 = jnp.bfloat16

D_MODEL = 1024
CHUNK = 64
SUB = 16
N_SUB = CHUNK // SUB
H_A = 8
DK_A = 128
DV_A = 128
D_A = H_A * DK_A
H_B = 8
DH_B = 64
N_KV_B = 4
G_B = H_B // N_KV_B
HEAD_W = 2 * DH_B
ROT_DIM = DH_B // 4
ROT_HALF = ROT_DIM // 2
ROPE_THETA = 500000.0
LAMBDA_INIT_SCALE = 0.3
D_FF = 4 * D_MODEL
FF_BLOCK = 1024
EPS = 1e-6
NEG_BIG = -1e30
LANE = 128
LOG2_E = math.log2(math.e)

VMEM_LIMIT = 52 * 1024 * 1024


def _rms(x):
    return x * lax.rsqrt(jnp.mean(x * x, axis=-1, keepdims=True) + EPS)


def _sigmoid(x):
    return 1.0 / (1.0 + jnp.exp(-x))


def _dot(a, b):
    return jnp.dot(a, b, preferred_element_type=F32)


def _dot_nt(a, b):
    return lax.dot_general(a, b, (((1,), (1,)), ((), ())), preferred_element_type=F32)


def _const_spec(shape):
    nd = len(shape)
    return pl.BlockSpec(shape, lambda *_: (0,) * nd)


def _cumsum_rows(tri, g):
    hi = g.astype(BF16)
    r1 = g - hi.astype(F32)
    mid = r1.astype(BF16)
    lo = (r1 - mid.astype(F32)).astype(BF16)
    return _dot(tri, hi) + _dot(tri, mid) + _dot(tri, lo)


def _hgrn_head_chunk(q, k, g, v, st):
    rows = lax.broadcasted_iota(jnp.int32, (CHUNK, CHUNK), 0)
    cols = lax.broadcasted_iota(jnp.int32, (CHUNK, CHUNK), 1)
    tri = (rows >= cols).astype(BF16)
    b = _cumsum_rows(tri, g)

    qe = q * jnp.exp(b)
    o = _dot_nt(qe.astype(BF16), st.astype(BF16))

    refs = [b[SUB * i - 1:SUB * i, :] for i in range(1, N_SUB)]
    ref_rows = jnp.concatenate(
        [jnp.zeros((SUB, DK_A), F32)]
        + [jnp.broadcast_to(r, (SUB, DK_A)) for r in refs], axis=0)
    qd = q * jnp.exp(b - ref_rows)
    kd = jnp.concatenate(
        [k[:SUB * i] * jnp.exp(refs[i - 1] - b[:SUB * i]) for i in range(1, N_SUB)], axis=0)
    v_all = jnp.concatenate([v[:SUB * i] for i in range(1, N_SUB)], axis=0)
    n_off = kd.shape[0]
    sc = _dot_nt(qd.astype(BF16), kd.astype(BF16))
    r_blk = lax.broadcasted_iota(jnp.int32, (CHUNK, n_off), 0) // SUB
    c_idx = lax.broadcasted_iota(jnp.int32, (CHUNK, n_off), 1)
    c_grp = jnp.zeros((CHUNK, n_off), jnp.int32)
    start = 0
    for i in range(1, N_SUB):
        c_grp = jnp.where(c_idx >= start, i, c_grp)
        start += SUB * i
    sc = jnp.where(r_blk == c_grp, sc, 0.0)
    o = o + _dot(sc.astype(BF16), v_all.astype(BF16))

    t_idx = lax.broadcasted_iota(jnp.int32, (SUB, 1), 0)
    diag = []
    for i in range(N_SUB):
        sl = slice(SUB * i, SUB * (i + 1))
        b_r, q_r, k_r, v_r = b[sl], q[sl], k[sl], v[sl]
        acc = jnp.zeros((SUB, DV_A), F32)
        for s in range(SUB):
            dec = jnp.exp(jnp.minimum(b_r - b_r[s:s + 1, :], 0.0))
            col = jnp.sum(q_r * dec * k_r[s:s + 1, :], axis=-1, keepdims=True)
            col = jnp.where(t_idx >= s, col, 0.0)
            acc = acc + col * v_r[s:s + 1, :]
        diag.append(acc)
    o = o + jnp.concatenate(diag, axis=0)

    b_last = b[CHUNK - 1:CHUNK, :]
    kdl = k * jnp.exp(b_last - b)
    st_new = st * jnp.exp(b_last) + _dot(v.T.astype(BF16), kdl.astype(BF16))
    return o, st_new


def _hgrn_kernel(x_ref, s0_ref, gpre_ref, win_ref, lbl_ref, gg_ref,
                 o_ref, sout_ref,
                 q_s, k_s, g_s, v_s, gate_s, o_s, st_s, *, tile):
    t = pl.program_id(1)

    @pl.when(t == 0)
    def _():
        for h in range(H_A):
            st_s[h] = s0_ref[0, h].T

    lbl = lbl_ref[...]
    e = jnp.exp(lbl - jnp.max(lbl, axis=0, keepdims=True))
    lb = e[0:1, :] / jnp.sum(e, axis=0, keepdims=True)

    u = (_rms(x_ref[0]) * gpre_ref[...]).astype(BF16)
    qf = _dot(u, win_ref[:, 0 * D_A:1 * D_A])
    ff = _dot(u, win_ref[:, 1 * D_A:2 * D_A])
    vf = _dot(u, win_ref[:, 2 * D_A:3 * D_A])
    gf = _dot(u, win_ref[:, 3 * D_A:4 * D_A])
    forget = lb + (1.0 - lb) * _sigmoid(ff)
    q = qf * _sigmoid(qf) * (DK_A ** -0.5)
    k = 1.0 - forget
    g = jnp.log(forget)
    gate = gf * _sigmoid(gf)
    for h in range(H_A):
        sl = slice(h * DK_A, (h + 1) * DK_A)
        q_s[h] = q[:, sl]
        k_s[h] = k[:, sl]
        g_s[h] = g[:, sl]
        v_s[h] = vf[:, sl]
        gate_s[h] = gate[:, sl]

    gg = gg_ref[...]

    def chunk_body(c, carry):
        r0 = pl.multiple_of(c * CHUNK, CHUNK)
        rs = pl.ds(r0, CHUNK)

        def head_body(h, carry2):
            o, st_new = _hgrn_head_chunk(q_s[h, rs, :], k_s[h, rs, :], g_s[h, rs, :],
                                         v_s[h, rs, :], st_s[h])
            st_s[h] = st_new
            o_s[h, rs, :] = _rms(o) * gg * gate_s[h, rs, :]
            return carry2

        return lax.fori_loop(0, H_A, head_body, carry)

    lax.fori_loop(0, tile // CHUNK, chunk_body, 0)

    for h in range(H_A):
        o_ref[0, :, h * DV_A:(h + 1) * DV_A] = o_s[h].astype(BF16)

    @pl.when(t == pl.num_programs(1) - 1)
    def _():
        for h in range(H_A):
            sout_ref[0, h] = st_s[h].T


def _hgrn_call(x, s0, gpre, w_in, lb_logits, ggain, *, tile):
    B, L, _ = x.shape
    grid = (B, L // tile)
    head_scratch = pltpu.VMEM((H_A, tile, DK_A), F32)
    return pl.pallas_call(
        functools.partial(_hgrn_kernel, tile=tile),
        grid=grid,
        in_specs=[
            pl.BlockSpec((1, tile, D_MODEL), lambda b, t: (b, t, 0)),
            pl.BlockSpec((1, H_A, DK_A, DV_A), lambda b, t: (b, 0, 0, 0)),
            _const_spec((1, D_MODEL)),
            _const_spec((D_MODEL, 4 * D_A)),
            _const_spec(lb_logits.shape),
            _const_spec((1, DV_A)),
        ],
        out_specs=[
            pl.BlockSpec((1, tile, D_A), lambda b, t: (b, t, 0)),
            pl.BlockSpec((1, H_A, DK_A, DV_A), lambda b, t: (b, 0, 0, 0)),
        ],
        out_shape=[
            jax.ShapeDtypeStruct((B, L, D_A), BF16),
            jax.ShapeDtypeStruct((B, H_A, DK_A, DV_A), F32),
        ],
        scratch_shapes=[head_scratch] * 6 + [pltpu.VMEM((H_A, DV_A, DK_A), F32)],
        compiler_params=pltpu.CompilerParams(
            dimension_semantics=("arbitrary", "arbitrary"),
            vmem_limit_bytes=VMEM_LIMIT),
        name="hgrn",
    )(x, s0, gpre, w_in, lb_logits, ggain)


def _mlp_kernel(x_ref, a_ref, wo_ref, gpost_ref, gpre_ref, wup_ref, wdn_ref, gmpost_ref, y_ref):
    h = x_ref[...] + _rms(_dot(a_ref[...], wo_ref[...])) * gpost_ref[...]
    hn = (_rms(h) * gpre_ref[...]).astype(BF16)
    acc = jnp.zeros(h.shape, F32)
    for c in range(D_FF // FF_BLOCK):
        sl = slice(c * FF_BLOCK, (c + 1) * FF_BLOCK)
        u = jnp.maximum(_dot(hn, wup_ref[:, sl]), 0.0)
        acc = acc + _dot((u * u).astype(BF16), wdn_ref[sl, :])
    y_ref[...] = h + _rms(acc) * gmpost_ref[...]


def _mlp_call(x, a, w_o, gpost, gpre, w_up, w_down, gmpost, *, tile):
    n = x.shape[0]
    row = lambda i: (i, 0)
    return pl.pallas_call(
        _mlp_kernel,
        grid=(n // tile,),
        in_specs=[
            pl.BlockSpec((tile, D_MODEL), row),
            pl.BlockSpec((tile, D_MODEL), row),
            _const_spec((D_MODEL, D_MODEL)),
            _const_spec((1, D_MODEL)),
            _const_spec((1, D_MODEL)),
            _const_spec((D_MODEL, D_FF)),
            _const_spec((D_FF, D_MODEL)),
            _const_spec((1, D_MODEL)),
        ],
        out_specs=pl.BlockSpec((tile, D_MODEL), row),
        out_shape=jax.ShapeDtypeStruct((n, D_MODEL), F32),
        compiler_params=pltpu.CompilerParams(
            dimension_semantics=("arbitrary",),
            vmem_limit_bytes=VMEM_LIMIT),
        name="mlp",
    )(x, a, w_o, gpost, gpre, w_up, w_down, gmpost)


def _rope_tables(pos):
    inv_freq = ROPE_THETA ** (-jnp.arange(0, ROT_DIM, 2, dtype=F32) / ROT_DIM)
    ang = pos.astype(F32)[:, None] * inv_freq[None, :]
    cos, sin = jnp.cos(ang), jnp.sin(ang)
    n = pos.shape[0]
    rest = DH_B - ROT_DIM
    one = jnp.ones((n, rest), F32)
    zero = jnp.zeros((n, rest), F32)
    zh = jnp.zeros((n, ROT_HALF), F32)
    c64 = jnp.concatenate([cos, cos, one], axis=1)
    up64 = jnp.concatenate([-sin, zh, zero], axis=1)
    dn64 = jnp.concatenate([zh, sin, zero], axis=1)
    twice = lambda t: jnp.concatenate([t, t], axis=1)
    return twice(c64), twice(up64), twice(dn64)


def _qkv_kernel(h_ref, gkv_ref, gq_ref, wkv_ref, wq_ref, cos_ref, up_ref, dn_ref,
                k_ref, v_ref, q_ref, *resident):
    xh = _rms(h_ref[...])
    kv = _dot((xh * gkv_ref[...]).astype(BF16), wkv_ref[...])
    q = _dot((xh * gq_ref[...]).astype(BF16), wq_ref[...])
    cos, up, dn = cos_ref[...], up_ref[...], dn_ref[...]

    def rope(x):
        return (x * cos + pltpu.roll(x, LANE - ROT_HALF, axis=1) * up
                + pltpu.roll(x, ROT_HALF, axis=1) * dn)

    n_k = N_KV_B * HEAD_W
    for j in range(N_KV_B):
        kj = rope(kv[:, j * HEAD_W:(j + 1) * HEAD_W])
        vj = kv[:, n_k + j * HEAD_W:n_k + (j + 1) * HEAD_W]
        k_ref[0, :, j, :] = kj
        v_ref[0, :, j, :] = vj
        if resident:
            kb_ref, vt_ref = resident
            kb_ref[0, j] = kj.astype(BF16)
            vt_ref[0, j] = vj.T.astype(BF16)
    for j in range(H_B):
        sl = slice(j * HEAD_W, (j + 1) * HEAD_W)
        q_ref[:, sl] = (rope(q[:, sl]) * (DH_B ** -0.5 * LOG2_E)).astype(BF16)


def _qkv_call(h, gkv, gq, w_kv, w_q, tables, *, batch, tile, resident):
    n = h.shape[0]
    seq = n // batch
    per_seq = max(1, seq // tile)
    seq_per_tile = max(1, tile // seq)
    rows = tile // seq_per_tile
    n_tab = tables[0].shape[0] // tile
    row = lambda i: (i, 0)
    tab = pl.BlockSpec((tile, LANE), lambda i: (i % n_tab, 0))
    n_k = N_KV_B * HEAD_W
    assert seq_per_tile == 1 or not resident
    if seq_per_tile == 1:
        kv_shape = (batch, seq, N_KV_B, HEAD_W)
        kv_spec = pl.BlockSpec((1, tile, N_KV_B, HEAD_W), lambda i: (i // per_seq, i % per_seq, 0, 0))
    else:
        kv_shape = (n // tile, tile, N_KV_B, HEAD_W)
        kv_spec = pl.BlockSpec((1, tile, N_KV_B, HEAD_W), lambda i: (i, 0, 0, 0))
    out_specs = [kv_spec, kv_spec, pl.BlockSpec((tile, H_B * HEAD_W), row)]
    out_shape = [
        jax.ShapeDtypeStruct(kv_shape, F32),
        jax.ShapeDtypeStruct(kv_shape, F32),
        jax.ShapeDtypeStruct((n, H_B * HEAD_W), BF16),
    ]
    if resident:
        out_specs += [
            pl.BlockSpec((1, N_KV_B, tile, HEAD_W), lambda i: (i // per_seq, 0, i % per_seq, 0)),
            pl.BlockSpec((1, N_KV_B, HEAD_W, tile), lambda i: (i // per_seq, 0, 0, i % per_seq)),
        ]
        out_shape += [
            jax.ShapeDtypeStruct((batch, N_KV_B, seq, HEAD_W), BF16),
            jax.ShapeDtypeStruct((batch, N_KV_B, HEAD_W, seq), BF16),
        ]
    return pl.pallas_call(
        _qkv_kernel,
        grid=(n // tile,),
        in_specs=[
            pl.BlockSpec((tile, D_MODEL), row),
            _const_spec((1, D_MODEL)),
            _const_spec((1, D_MODEL)),
            _const_spec((D_MODEL, 2 * n_k)),
            _const_spec((D_MODEL, H_B * HEAD_W)),
            tab, tab, tab,
        ],
        out_specs=out_specs,
        out_shape=out_shape,
        compiler_params=pltpu.CompilerParams(
            dimension_semantics=("arbitrary",),
            vmem_limit_bytes=VMEM_LIMIT),
        name="qkv",
    )(h, gkv, gq, w_kv, w_q, *tables)


N_STACK = 2 * G_B


def _stacked_q_t(q, tq):
    lane = lax.broadcasted_iota(jnp.int32, (tq, HEAD_W), 1)
    parts = []
    for g in range(G_B):
        qg = q[:, g * HEAD_W:(g + 1) * HEAD_W].astype(F32)
        parts.append(jnp.where(lane < DH_B, qg, 0.0))
        parts.append(jnp.where(lane >= DH_B, qg, 0.0))
    return jnp.concatenate(parts, axis=0).T.astype(BF16)


def _attn_step(k_blk, vt_blk, qst, acc, m_prev, l_prev, mask):
    s = _dot(k_blk, qst[...])
    if mask is not None:
        s = jnp.where(mask, s, NEG_BIG)
    m_new = jnp.maximum(m_prev, jnp.max(s, axis=0, keepdims=True))
    alpha = jnp.exp2(m_prev - m_new)
    p = jnp.exp2(s - m_new)
    l_new = alpha * l_prev + jnp.sum(p, axis=0, keepdims=True)
    acc[...] = acc[...] * alpha + _dot(vt_blk, p.astype(BF16))
    return m_new, l_new


def _chunk_mask(q_pos0, k_pos0, tq, tk):
    c = lax.broadcasted_iota(jnp.int32, (1, N_STACK * tq), 1)
    qc = (q_pos0 + c % tq) // CHUNK
    kc = (k_pos0 + lax.broadcasted_iota(jnp.int32, (tk, 1), 0)) // CHUNK
    return kc <= qc


def _lambda(lq1, lk1, lq2, lk2, lam_init):
    return (jnp.exp(jnp.sum(lq1[...] * lk1[...], axis=1, keepdims=True))
            - jnp.exp(jnp.sum(lq2[...] * lk2[...], axis=1, keepdims=True)) + lam_init)


def _attn_finish(store, acc, l, lam, sub, tq, lam_init):
    o = (acc * (1.0 / l)).T
    for g in range(G_B):
        og = o[(2 * g) * tq:(2 * g + 1) * tq] - lam * o[(2 * g + 1) * tq:(2 * g + 2) * tq]
        store(g, (_rms(og) * sub * (1.0 - lam_init)).astype(BF16))


def _attn_prompt_kernel(q_ref, kb_ref, vt_ref, lq1, lk1, lq2, lk2, sub_ref, o_ref,
                        qst, acc, *, tq, tk, lam_init):
    i = pl.program_id(2)
    n = N_STACK * tq
    qst[...] = _stacked_q_t(q_ref[0], tq)
    acc[...] = jnp.zeros(acc.shape, F32)

    def step(j, carry, masked):
        r0 = pl.multiple_of(j * tk, tk)
        mask = _chunk_mask(i * tq, r0, tq, tk) if masked else None
        return _attn_step(kb_ref[0, 0, pl.ds(r0, tk), :], vt_ref[0, 0, :, pl.ds(r0, tk)],
                          qst, acc, carry[0], carry[1], mask)

    n_full = (i * tq + CHUNK) // tk
    n_need = ((i + 1) * tq + tk - 1) // tk
    carry = (jnp.full((1, n), NEG_BIG, F32), jnp.zeros((1, n), F32))
    carry = lax.fori_loop(0, n_full, functools.partial(step, masked=False), carry)
    _, l = lax.fori_loop(n_full, n_need, functools.partial(step, masked=True), carry)

    def store(g, og):
        o_ref[0, :, g * HEAD_W:(g + 1) * HEAD_W] = og

    _attn_finish(store, acc[...], l, _lambda(lq1, lk1, lq2, lk2, lam_init), sub_ref[...],
                 tq, lam_init)


def _attn_prompt_call(q, kb, vt, lam_params, sub_gain, *, tq, tk, lam_init):
    B, L, _ = q.shape
    assert L % tq == 0 and L % tk == 0 and tq % CHUNK == 0
    n = N_STACK * tq
    q_spec = pl.BlockSpec((1, tq, G_B * HEAD_W), lambda b, h, i: (b, i, h))
    small = [_const_spec((1, DH_B))] * 4 + [_const_spec((1, HEAD_W))]
    return pl.pallas_call(
        functools.partial(_attn_prompt_kernel, tq=tq, tk=tk, lam_init=lam_init),
        grid=(B, N_KV_B, L // tq),
        in_specs=[
            q_spec,
            pl.BlockSpec((1, 1, L, HEAD_W), lambda b, h, i: (b, h, 0, 0)),
            pl.BlockSpec((1, 1, HEAD_W, L), lambda b, h, i: (b, h, 0, 0)),
        ] + small,
        out_specs=q_spec,
        out_shape=jax.ShapeDtypeStruct((B, L, H_B * HEAD_W), BF16),
        scratch_shapes=[pltpu.VMEM((HEAD_W, n), BF16), pltpu.VMEM((HEAD_W, n), F32)],
        compiler_params=pltpu.CompilerParams(
            dimension_semantics=("arbitrary",) * 3,
            vmem_limit_bytes=VMEM_LIMIT),
        name="attn_prompt",
    )(q, kb, vt, *lam_params, sub_gain)


def _attn_sample_kernel(q_ref, pk_ref, pv_ref, nk_ref, nv_ref, lq1, lk1, lq2, lk2, sub_ref, o_ref,
                        qst, acc, m_s, l_s, *, tq, n_past, past_len, lam_init):
    j = pl.program_id(1)

    @pl.when(j == 0)
    def _():
        q = q_ref[0]
        for h in range(N_KV_B):
            qst[h] = _stacked_q_t(q[:, h * G_B * HEAD_W:(h + 1) * G_B * HEAD_W], tq)
        acc[...] = jnp.zeros(acc.shape, F32)
        m_s[...] = jnp.full(m_s.shape, NEG_BIG, F32)
        l_s[...] = jnp.zeros(l_s.shape, F32)

    def run(k_ref, v_ref, mask):
        for h in range(N_KV_B):
            m_new, l_new = _attn_step(k_ref[0, :, h, :].astype(BF16),
                                      v_ref[0, :, h, :].T.astype(BF16),
                                      qst.at[h], acc.at[h], m_s[h], l_s[h], mask)
            m_s[h] = m_new
            l_s[h] = l_new

    @pl.when(j < n_past)
    def _():
        run(pk_ref, pv_ref, None)

    @pl.when(j == n_past)
    def _():
        run(nk_ref, nv_ref, _chunk_mask(past_len, past_len, tq, tq))
        lam = _lambda(lq1, lk1, lq2, lk2, lam_init)
        for h in range(N_KV_B):
            def store(g, og, h=h):
                c0 = (h * G_B + g) * HEAD_W
                o_ref[0, :, c0:c0 + HEAD_W] = og
            _attn_finish(store, acc[h], l_s[h], lam, sub_ref[...], tq, lam_init)


def _attn_sample_call(q, past_k, past_v, k_new, v_new, lam_params, sub_gain, *, tk, lam_init):
    B, tq, _ = q.shape
    past_len = past_k.shape[1]
    assert past_len % tk == 0 and past_len % CHUNK == 0 and tq <= CHUNK
    n_past = past_len // tk
    n = N_STACK * tq
    past_spec = pl.BlockSpec((1, tk, N_KV_B, HEAD_W), lambda b, j: (b, jnp.minimum(j, n_past - 1), 0, 0))
    new_spec = pl.BlockSpec((1, tq, N_KV_B, HEAD_W), lambda b, j: (b, 0, 0, 0))
    q_spec = pl.BlockSpec((1, tq, H_B * HEAD_W), lambda b, j: (b, 0, 0))
    small = [_const_spec((1, DH_B))] * 4 + [_const_spec((1, HEAD_W))]
    return pl.pallas_call(
        functools.partial(_attn_sample_kernel, tq=tq, n_past=n_past,
                          past_len=past_len, lam_init=lam_init),
        grid=(B, n_past + 1),
        in_specs=[q_spec, past_spec, past_spec, new_spec, new_spec] + small,
        out_specs=q_spec,
        out_shape=jax.ShapeDtypeStruct((B, tq, H_B * HEAD_W), BF16),
        scratch_shapes=[
            pltpu.VMEM((N_KV_B, HEAD_W, n), BF16),
            pltpu.VMEM((N_KV_B, HEAD_W, n), F32),
            pltpu.VMEM((N_KV_B, 1, n), F32),
            pltpu.VMEM((N_KV_B, 1, n), F32),
        ],
        compiler_params=pltpu.CompilerParams(
            dimension_semantics=("arbitrary",) * 2,
            vmem_limit_bytes=VMEM_LIMIT),
        name="attn_sample",
    )(q, past_k, past_v, k_new, v_new, *lam_params, sub_gain)


def _pick_tile(n, want):
    t = min(n, want)
    assert n % t == 0
    return t


def kernel(x_prompt, x_sample, state_hgrn, cache_k, cache_v, norm_mix_pre, norm_mix_post,
           norm_mlp_pre, norm_mlp_post, w_up, w_down, w_in_a, lb_logits, gnorm_a, w_out_a,
           norm_kv, w_kv, w_q_b, lam_q1, lam_k1, lam_q2, lam_k2, subln_b, w_out_b):
    row = lambda a: a.reshape(1, -1)
    w_in_bf = w_in_a[0].astype(BF16)
    w_out_a_bf = w_out_a[0].astype(BF16)
    w_up_bf = w_up.astype(BF16)
    w_down_bf = w_down.astype(BF16)
    w_kv_bf = w_kv.astype(BF16)
    w_q_bf = w_q_b[0].astype(BF16)
    w_out_b_bf = w_out_b[0].astype(BF16)
    lam_params = [row(lam_q1[0]), row(lam_k1[0]), row(lam_q2[0]), row(lam_k2[0])]
    lam_init = 0.8 - 0.6 * math.exp(-LAMBDA_INIT_SCALE * 1)

    def mlp(layer, x, a, w_o):
        return _mlp_call(x, a, w_o, row(norm_mix_post[layer]), row(norm_mlp_pre[layer]),
                         w_up_bf[layer], w_down_bf[layer], row(norm_mlp_post[layer]),
                         tile=_pick_tile(x.shape[0], 512))

    def trunk(x, s0, past_k, past_v):
        B, L, _ = x.shape
        n = B * L
        o_a, s_out = _hgrn_call(x, s0, row(norm_mix_pre[0]), w_in_bf, lb_logits, row(gnorm_a[0]),
                                tile=_pick_tile(L, 256))
        h = mlp(0, x.reshape(n, D_MODEL), o_a.reshape(n, D_A), w_out_a_bf)

        pos0 = 0 if past_k is None else past_k.shape[1]
        tile = _pick_tile(n, 512)
        reps = max(1, tile // L)
        tables = _rope_tables(pos0 + jnp.arange(L, dtype=jnp.int32))
        tables = [jnp.tile(t, (reps, 1)) for t in tables]
        outs = _qkv_call(h, row(norm_kv), row(norm_mix_pre[1]), w_kv_bf, w_q_bf, tables,
                         batch=B, tile=tile, resident=past_k is None)
        k, v, q = outs[:3]
        k = k.reshape(B, L, N_KV_B, HEAD_W)
        v = v.reshape(B, L, N_KV_B, HEAD_W)
        q3 = q.reshape(B, L, H_B * HEAD_W)
        if past_k is None:
            o_b = _attn_prompt_call(q3, outs[3], outs[4], lam_params, row(subln_b[0]),
                                    tq=256, tk=512, lam_init=lam_init)
        else:
            o_b = _attn_sample_call(q3, past_k, past_v, k, v, lam_params, row(subln_b[0]),
                                    tk=512, lam_init=lam_init)
        y = mlp(1, h, o_b.reshape(n, D_MODEL), w_out_b_bf)
        return y.reshape(B, L, D_MODEL), s_out[None], k, v

    s0_prompt = jnp.zeros((x_prompt.shape[0], H_A, DK_A, DV_A), F32)
    y_p, s_p, k_p, v_p = trunk(x_prompt, s0_prompt, None, None)
    y_s, s_s, k_s, v_s = trunk(x_sample, state_hgrn[0], cache_k, cache_v)
    return (y_p, y_s, s_p, k_p, v_p, s_s, k_s, v_s)
```

```python
import functools
import math

import jax
import jax.numpy as jnp
from jax import lax
from jax.experimental import pallas as pl
from jax.experimental.pallas import tpu as pltpu

F32 = jnp.float32
BF16 = jnp.bfloat16

D_MODEL = 1024
CHUNK = 64
SUB = 8
N_SUB = CHUNK // SUB
H_A = 8
DK_A = 128
DV_A = 128
D_A = H_A * DK_A
H_B = 8
DH_B = 64
N_KV_B = 4
G_B = H_B // N_KV_B
HEAD_W = 2 * DH_B
ROT_DIM = DH_B // 4
ROT_HALF = ROT_DIM // 2
ROPE_THETA = 500000.0
LAMBDA_INIT_SCALE = 0.3
D_FF = 4 * D_MODEL
FF_BLOCK = 1024
EPS = 1e-6
NEG_BIG = -1e30
LANE = 128
LOG2_E = math.log2(math.e)

VMEM_LIMIT = 52 * 1024 * 1024


def _rms(x):
    return x * lax.rsqrt(jnp.mean(x * x, axis=-1, keepdims=True) + EPS)


def _sigmoid(x):
    return 1.0 / (1.0 + jnp.exp(-x))


def _dot(a, b):
    return jnp.dot(a, b, preferred_element_type=F32)


def _dot_nt(a, b):
    return lax.dot_general(a, b, (((1,), (1,)), ((), ())), preferred_element_type=F32)


def _const_spec(shape):
    nd = len(shape)
    return pl.BlockSpec(shape, lambda *_: (0,) * nd)


def _cumsum_rows(tri, g):
    hi = g.astype(BF16)
    r1 = g - hi.astype(F32)
    mid = r1.astype(BF16)
    lo = (r1 - mid.astype(F32)).astype(BF16)
    return _dot(tri, hi) + _dot(tri, mid) + _dot(tri, lo)


def _hgrn_consts():
    rows = lax.broadcasted_iota(jnp.int32, (CHUNK, CHUNK), 0)
    cols = lax.broadcasted_iota(jnp.int32, (CHUNK, CHUNK), 1)
    tri = (rows >= cols).astype(BF16)
    n_off = SUB * (N_SUB * (N_SUB - 1) // 2)
    r_blk = lax.broadcasted_iota(jnp.int32, (CHUNK, n_off), 0) // SUB
    c_idx = lax.broadcasted_iota(jnp.int32, (CHUNK, n_off), 1)
    c_grp = jnp.zeros((CHUNK, n_off), jnp.int32)
    start = 0
    for i in range(1, N_SUB):
        c_grp = jnp.where(c_idx >= start, i, c_grp)
        start += SUB * i
    return tri, r_blk == c_grp


def _hgrn_chunk(q, k, g2, v, st, tri, own_group):
    heads = [slice(h * DK_A, (h + 1) * DK_A) for h in range(H_A)]
    b = _cumsum_rows(tri, g2)
    b_last = b[CHUNK - 1:CHUNK, :]

    refs = [b[SUB * i - 1:SUB * i, :] for i in range(1, N_SUB)]
    ref_rows = jnp.concatenate(
        [jnp.zeros((SUB, D_A), F32)] + [jnp.broadcast_to(r, (SUB, D_A)) for r in refs], axis=0)
    qe = (q * jnp.exp2(b)).astype(BF16)
    qd = (q * jnp.exp2(b - ref_rows)).astype(BF16)
    kd = jnp.concatenate(
        [k[:SUB * i] * jnp.exp2(refs[i - 1] - b[:SUB * i]) for i in range(1, N_SUB)],
        axis=0).astype(BF16)
    v_all = jnp.concatenate([v[:SUB * i] for i in range(1, N_SUB)], axis=0).astype(BF16)
    kdl = (k * jnp.exp2(b_last - b)).astype(BF16)
    keep = jnp.exp2(b_last)

    o = [_dot_nt(qe[:, hs], st[h].astype(BF16)) for h, hs in enumerate(heads)]
    sc = [_dot_nt(qd[:, hs], kd[:, hs]) for hs in heads]
    sc = [jnp.where(own_group, x, 0.0).astype(BF16) for x in sc]
    o = [o[h] + _dot(sc[h], v_all[:, hs]) for h, hs in enumerate(heads)]
    st_new = [st[h] * keep[:, hs] + _dot(v[:, hs].T.astype(BF16), kdl[:, hs])
              for h, hs in enumerate(heads)]

    t_idx = lax.broadcasted_iota(jnp.int32, (SUB, 1), 0)
    diag = [[] for _ in heads]
    for i in range(N_SUB):
        sl = slice(SUB * i, SUB * (i + 1))
        b_r, q_r, k_r, v_r = b[sl], q[sl], k[sl], v[sl]
        acc = [jnp.zeros((SUB, DV_A), F32) for _ in heads]
        for s in range(SUB):
            w = q_r * jnp.exp2(b_r - b_r[s:s + 1, :]) * k_r[s:s + 1, :]
            for h, hs in enumerate(heads):
                col = jnp.sum(w[:, hs], axis=-1, keepdims=True)
                col = jnp.where(t_idx >= s, col, 0.0)
                acc[h] = acc[h] + col * v_r[s:s + 1, hs]
        for h in range(H_A):
            diag[h].append(acc[h])
    o = [o[h] + jnp.concatenate(diag[h], axis=0) for h in range(H_A)]
    return o, st_new


def _hgrn_kernel(x_ref, s0_ref, gpre_ref, win_ref, lbl_ref, gg_ref,
                 o_ref, sout_ref,
                 q_s, k_s, g_s, v_s, gate_s, st_s, *, tile):
    t = pl.program_id(1)

    @pl.when(t == 0)
    def _():
        for h in range(H_A):
            st_s[h] = s0_ref[0, h].T

    lbl = lbl_ref[...]
    e = jnp.exp(lbl - jnp.max(lbl, axis=0, keepdims=True))
    lb = e[0:1, :] / jnp.sum(e, axis=0, keepdims=True)

    u = (_rms(x_ref[0]) * gpre_ref[...]).astype(BF16)
    qf = _dot(u, win_ref[:, 0 * D_A:1 * D_A])
    ff = _dot(u, win_ref[:, 1 * D_A:2 * D_A])
    vf = _dot(u, win_ref[:, 2 * D_A:3 * D_A])
    gf = _dot(u, win_ref[:, 3 * D_A:4 * D_A])
    forget = lb + (1.0 - lb) * _sigmoid(ff)
    q = qf * _sigmoid(qf) * (DK_A ** -0.5)
    q_s[...] = q
    k_s[...] = 1.0 - forget
    g_s[...] = jnp.log2(forget)
    v_s[...] = vf
    gate_s[...] = gf * _sigmoid(gf)

    gg = gg_ref[...]
    tri, own_group = _hgrn_consts()

    def chunk_body(c, carry):
        rs = pl.ds(pl.multiple_of(c * CHUNK, CHUNK), CHUNK)
        o, st_new = _hgrn_chunk(q_s[rs, :], k_s[rs, :], g_s[rs, :], v_s[rs, :],
                                [st_s[h] for h in range(H_A)], tri, own_group)
        for h in range(H_A):
            hs = slice(h * DV_A, (h + 1) * DV_A)
            st_s[h] = st_new[h]
            o_ref[0, rs, hs] = (_rms(o[h]) * gg * gate_s[rs, hs]).astype(BF16)
        return carry

    lax.fori_loop(0, tile // CHUNK, chunk_body, 0)

    @pl.when(t == pl.num_programs(1) - 1)
    def _():
        for h in range(H_A):
            sout_ref[0, h] = st_s[h].T


def _hgrn_call(x, s0, gpre, w_in, lb_logits, ggain, *, tile):
    B, L, _ = x.shape
    grid = (B, L // tile)
    row_scratch = pltpu.VMEM((tile, D_A), F32)
    return pl.pallas_call(
        functools.partial(_hgrn_kernel, tile=tile),
        grid=grid,
        in_specs=[
            pl.BlockSpec((1, tile, D_MODEL), lambda b, t: (b, t, 0)),
            pl.BlockSpec((1, H_A, DK_A, DV_A), lambda b, t: (b, 0, 0, 0)),
            _const_spec((1, D_MODEL)),
            _const_spec((D_MODEL, 4 * D_A)),
            _const_spec(lb_logits.shape),
            _const_spec((1, DV_A)),
        ],
        out_specs=[
            pl.BlockSpec((1, tile, D_A), lambda b, t: (b, t, 0)),
            pl.BlockSpec((1, H_A, DK_A, DV_A), lambda b, t: (b, 0, 0, 0)),
        ],
        out_shape=[
            jax.ShapeDtypeStruct((B, L, D_A), BF16),
            jax.ShapeDtypeStruct((B, H_A, DK_A, DV_A), F32),
        ],
        scratch_shapes=[row_scratch] * 5 + [pltpu.VMEM((H_A, DV_A, DK_A), F32)],
        compiler_params=pltpu.CompilerParams(
            dimension_semantics=("arbitrary", "arbitrary"),
            vmem_limit_bytes=VMEM_LIMIT),
        name="hgrn",
    )(x, s0, gpre, w_in, lb_logits, ggain)


def _mlp_kernel(x_ref, a_ref, wo_ref, gpost_ref, gpre_ref, wup_ref, wdn_ref, gmpost_ref, y_ref):
    h = x_ref[...] + _rms(_dot(a_ref[...], wo_ref[...])) * gpost_ref[...]
    hn = (_rms(h) * gpre_ref[...]).astype(BF16)
    acc = jnp.zeros(h.shape, F32)
    for c in range(D_FF // FF_BLOCK):
        sl = slice(c * FF_BLOCK, (c + 1) * FF_BLOCK)
        u = jnp.maximum(_dot(hn, wup_ref[:, sl]), 0.0)
        acc = acc + _dot((u * u).astype(BF16), wdn_ref[sl, :])
    y_ref[...] = h + _rms(acc) * gmpost_ref[...]


def _mlp_call(x, a, w_o, gpost, gpre, w_up, w_down, gmpost, *, tile):
    n = x.shape[0]
    row = lambda i: (i, 0)
    return pl.pallas_call(
        _mlp_kernel,
        grid=(n // tile,),
        in_specs=[
            pl.BlockSpec((tile, D_MODEL), row),
            pl.BlockSpec((tile, D_MODEL), row),
            _const_spec((D_MODEL, D_MODEL)),
            _const_spec((1, D_MODEL)),
            _const_spec((1, D_MODEL)),
            _const_spec((D_MODEL, D_FF)),
            _const_spec((D_FF, D_MODEL)),
            _const_spec((1, D_MODEL)),
        ],
        out_specs=pl.BlockSpec((tile, D_MODEL), row),
        out_shape=jax.ShapeDtypeStruct((n, D_MODEL), F32),
        compiler_params=pltpu.CompilerParams(
            dimension_semantics=("arbitrary",),
            vmem_limit_bytes=VMEM_LIMIT),
        name="mlp",
    )(x, a, w_o, gpost, gpre, w_up, w_down, gmpost)


def _rope_tables(pos):
    inv_freq = ROPE_THETA ** (-jnp.arange(0, ROT_DIM, 2, dtype=F32) / ROT_DIM)
    ang = pos.astype(F32)[:, None] * inv_freq[None, :]
    cos, sin = jnp.cos(ang), jnp.sin(ang)
    n = pos.shape[0]
    rest = DH_B - ROT_DIM
    one = jnp.ones((n, rest), F32)
    zero = jnp.zeros((n, rest), F32)
    zh = jnp.zeros((n, ROT_HALF), F32)
    c64 = jnp.concatenate([cos, cos, one], axis=1)
    up64 = jnp.concatenate([-sin, zh, zero], axis=1)
    dn64 = jnp.concatenate([zh, sin, zero], axis=1)
    twice = lambda t: jnp.concatenate([t, t], axis=1)
    return twice(c64), twice(up64), twice(dn64)


def _qkv_kernel(h_ref, gkv_ref, gq_ref, wkv_ref, wq_ref, cos_ref, up_ref, dn_ref,
                k_ref, v_ref, q_ref, *resident):
    tile = h_ref.shape[0]
    xh = _rms(h_ref[...])
    kv = _dot((xh * gkv_ref[...]).astype(BF16), wkv_ref[...])
    q = _dot((xh * gq_ref[...]).astype(BF16), wq_ref[...])
    cos, up, dn = cos_ref[...], up_ref[...], dn_ref[...]

    def rope(x):
        return (x * cos + pltpu.roll(x, LANE - ROT_HALF, axis=1) * up
                + pltpu.roll(x, ROT_HALF, axis=1) * dn)

    n_k = N_KV_B * HEAD_W
    for j in range(N_KV_B):
        kj = rope(kv[:, j * HEAD_W:(j + 1) * HEAD_W])
        vj = kv[:, n_k + j * HEAD_W:n_k + (j + 1) * HEAD_W]
        k_ref[pl.ds(j, tile, stride=N_KV_B), :] = kj
        v_ref[pl.ds(j, tile, stride=N_KV_B), :] = vj
        if resident:
            kb_ref, vt_ref = resident
            kb_ref[0, j] = kj.astype(BF16)
            vt_ref[0, j] = vj.T.astype(BF16)
    for j in range(H_B):
        sl = slice(j * HEAD_W, (j + 1) * HEAD_W)
        q_ref[:, sl] = (rope(q[:, sl]) * (DH_B ** -0.5 * LOG2_E)).astype(BF16)


def _qkv_call(h, gkv, gq, w_kv, w_q, tables, *, batch, tile, resident):
    n = h.shape[0]
    seq = n // batch
    per_seq = max(1, seq // tile)
    n_tab = tables[0].shape[0] // tile
    row = lambda i: (i, 0)
    tab = pl.BlockSpec((tile, LANE), lambda i: (i % n_tab, 0))
    n_k = N_KV_B * HEAD_W
    assert seq % tile == 0 or not resident
    kv_spec = pl.BlockSpec((tile * N_KV_B, HEAD_W), row)
    out_specs = [kv_spec, kv_spec, pl.BlockSpec((tile, H_B * HEAD_W), row)]
    out_shape = [
        jax.ShapeDtypeStruct((n * N_KV_B, HEAD_W), F32),
        jax.ShapeDtypeStruct((n * N_KV_B, HEAD_W), F32),
        jax.ShapeDtypeStruct((n, H_B * HEAD_W), BF16),
    ]
    if resident:
        out_specs += [
            pl.BlockSpec((1, N_KV_B, tile, HEAD_W), lambda i: (i // per_seq, 0, i % per_seq, 0)),
            pl.BlockSpec((1, N_KV_B, HEAD_W, tile), lambda i: (i // per_seq, 0, 0, i % per_seq)),
        ]
        out_shape += [
            jax.ShapeDtypeStruct((batch, N_KV_B, seq, HEAD_W), BF16),
            jax.ShapeDtypeStruct((batch, N_KV_B, HEAD_W, seq), BF16),
        ]
    return pl.pallas_call(
        _qkv_kernel,
        grid=(n // tile,),
        in_specs=[
            pl.BlockSpec((tile, D_MODEL), row),
            _const_spec((1, D_MODEL)),
            _const_spec((1, D_MODEL)),
            _const_spec((D_MODEL, 2 * n_k)),
            _const_spec((D_MODEL, H_B * HEAD_W)),
            tab, tab, tab,
        ],
        out_specs=out_specs,
        out_shape=out_shape,
        compiler_params=pltpu.CompilerParams(
            dimension_semantics=("arbitrary",),
            vmem_limit_bytes=VMEM_LIMIT),
        name="qkv",
    )(h, gkv, gq, w_kv, w_q, *tables)


N_STACK = 2 * G_B
COL_GROUP = 256


def _stacked_q_t(q, tq):
    lane = lax.broadcasted_iota(jnp.int32, (tq, HEAD_W), 1)
    parts = []
    for g in range(G_B):
        qg = q[:, g * HEAD_W:(g + 1) * HEAD_W].astype(F32)
        parts.append(jnp.where(lane < DH_B, qg, 0.0))
        parts.append(jnp.where(lane >= DH_B, qg, 0.0))
    return jnp.concatenate(parts, axis=0).T.astype(BF16)


def _attn_consume(s, vt_blk, acc, m_prev, l_prev, mask):
    if mask is not None:
        s = jnp.where(mask, s, NEG_BIG)
    m_new = jnp.maximum(m_prev, jnp.max(s, axis=0, keepdims=True))
    alpha = jnp.exp2(m_prev - m_new)
    p = jnp.exp2(s - m_new)
    l_new = alpha * l_prev + jnp.sum(p, axis=0, keepdims=True)
    p = p.astype(BF16)
    n = s.shape[1]
    for c in range(n // COL_GROUP):
        cs = slice(c * COL_GROUP, (c + 1) * COL_GROUP)
        acc[:, cs] = acc[:, cs] * alpha[:, cs] + _dot(vt_blk, p[:, cs])
    return m_new, l_new


def _chunk_mask(q_pos0, k_pos0, tq, tk):
    c = lax.broadcasted_iota(jnp.int32, (1, N_STACK * tq), 1)
    qc = (q_pos0 + c % tq) // CHUNK
    kc = (k_pos0 + lax.broadcasted_iota(jnp.int32, (tk, 1), 0)) // CHUNK
    return kc <= qc


def _lambda(lq1, lk1, lq2, lk2, lam_init):
    return (jnp.exp(jnp.sum(lq1[...] * lk1[...], axis=1, keepdims=True))
            - jnp.exp(jnp.sum(lq2[...] * lk2[...], axis=1, keepdims=True)) + lam_init)


def _attn_finish(store, acc, l, lam, sub, tq, lam_init):
    o = (acc * (1.0 / l)).T
    for g in range(G_B):
        og = o[(2 * g) * tq:(2 * g + 1) * tq] - lam * o[(2 * g + 1) * tq:(2 * g + 2) * tq]
        store(g, (_rms(og) * sub * (1.0 - lam_init)).astype(BF16))


def _attn_prompt_kernel(q_ref, kb_ref, vt_ref, lq1, lk1, lq2, lk2, sub_ref, o_ref,
                        qst, acc, s_a, s_b, ml_s, *, tq, tk, lam_init):
    i = pl.program_id(2)
    n = N_STACK * tq
    qst[...] = _stacked_q_t(q_ref[0], tq)
    acc[...] = jnp.zeros(acc.shape, F32)

    n_full = (i * tq + CHUNK) // tk
    n_need = ((i + 1) * tq + tk - 1) // tk

    def scores(j):
        r0 = pl.multiple_of(j * tk, tk)
        return _dot(kb_ref[0, 0, pl.ds(r0, tk), :], qst[...])

    def consume(buf, j, carry, masked):
        r0 = pl.multiple_of(j * tk, tk)
        mask = _chunk_mask(i * tq, r0, tq, tk) if masked else None
        return _attn_consume(buf[...], vt_ref[0, 0, :, pl.ds(r0, tk)], acc,
                             carry[0], carry[1], mask)

    def pair(jj, carry):
        j = 2 * jj
        s_b[...] = scores(j + 1)
        carry = consume(s_a, j, carry, False)
        s_a[...] = scores(j + 2)
        return consume(s_b, j + 1, carry, False)

    s_a[...] = scores(0)
    carry = (jnp.full((1, n), NEG_BIG, F32), jnp.zeros((1, n), F32))
    carry = lax.fori_loop(0, n_full // 2, pair, carry)
    j_tail = 2 * (n_full // 2)

    @pl.when(n_full % 2 == 1)
    def _():
        s_b[...] = scores(j_tail + 1)
        m, l = consume(s_a, j_tail, carry, False)
        m, l = consume(s_b, j_tail + 1, (m, l), True)
        ml_s[0:1, :] = m
        ml_s[1:2, :] = l

    @pl.when(n_full % 2 == 0)
    def _():
        m, l = consume(s_a, j_tail, carry, True)
        ml_s[0:1, :] = m
        ml_s[1:2, :] = l

    l = ml_s[1:2, :]

    def store(g, og):
        o_ref[0, :, g * HEAD_W:(g + 1) * HEAD_W] = og

    _attn_finish(store, acc[...], l, _lambda(lq1, lk1, lq2, lk2, lam_init), sub_ref[...],
                 tq, lam_init)


def _attn_prompt_call(q, kb, vt, lam_params, sub_gain, *, tq, tk, lam_init):
    B, L, _ = q.shape
    assert L % tq == 0 and L % tk == 0 and tq % CHUNK == 0
    assert all(((i + 1) * tq + tk - 1) // tk - (i * tq + CHUNK) // tk == 1 for i in range(L // tq))
    n = N_STACK * tq
    q_spec = pl.BlockSpec((1, tq, G_B * HEAD_W), lambda b, h, i: (b, i, h))
    small = [_const_spec((1, DH_B))] * 4 + [_const_spec((1, HEAD_W))]
    return pl.pallas_call(
        functools.partial(_attn_prompt_kernel, tq=tq, tk=tk, lam_init=lam_init),
        grid=(B, N_KV_B, L // tq),
        in_specs=[
            q_spec,
            pl.BlockSpec((1, 1, L, HEAD_W), lambda b, h, i: (b, h, 0, 0)),
            pl.BlockSpec((1, 1, HEAD_W, L), lambda b, h, i: (b, h, 0, 0)),
        ] + small,
        out_specs=q_spec,
        out_shape=jax.ShapeDtypeStruct((B, L, H_B * HEAD_W), BF16),
        scratch_shapes=[pltpu.VMEM((HEAD_W, n), BF16), pltpu.VMEM((HEAD_W, n), F32),
                        pltpu.VMEM((tk, n), F32), pltpu.VMEM((tk, n), F32),
                        pltpu.VMEM((2, n), F32)],
        compiler_params=pltpu.CompilerParams(
            dimension_semantics=("arbitrary",) * 3,
            vmem_limit_bytes=VMEM_LIMIT),
        name="attn_prompt",
    )(q, kb, vt, *lam_params, sub_gain)


def _attn_sample_kernel(q_ref, pk_ref, pv_ref, nk_ref, nv_ref, lq1, lk1, lq2, lk2, sub_ref, o_ref,
                        qst, acc, m_s, l_s, *, tq, n_past, past_len, lam_init):
    j = pl.program_id(1)

    @pl.when(j == 0)
    def _():
        q = q_ref[0]
        for h in range(N_KV_B):
            qst[h] = _stacked_q_t(q[:, h * G_B * HEAD_W:(h + 1) * G_B * HEAD_W], tq)
        acc[...] = jnp.zeros(acc.shape, F32)
        m_s[...] = jnp.full(m_s.shape, NEG_BIG, F32)
        l_s[...] = jnp.zeros(l_s.shape, F32)

    def run(k_ref, v_ref, mask):
        keys = k_ref.shape[1] // N_KV_B
        head = lambda ref, h: ref[0, pl.ds(h, keys, stride=N_KV_B), :]
        ss = [_dot(head(k_ref, h).astype(BF16), qst[h]) for h in range(N_KV_B)]
        vts = [head(v_ref, h).T.astype(BF16) for h in range(N_KV_B)]
        for h in range(N_KV_B):
            m_new, l_new = _attn_consume(ss[h], vts[h], acc.at[h], m_s[h], l_s[h], mask)
            m_s[h] = m_new
            l_s[h] = l_new

    @pl.when(j < n_past)
    def _():
        run(pk_ref, pv_ref, None)

    @pl.when(j == n_past)
    def _():
        run(nk_ref, nv_ref, _chunk_mask(past_len, past_len, tq, tq))
        lam = _lambda(lq1, lk1, lq2, lk2, lam_init)
        for h in range(N_KV_B):
            def store(g, og, h=h):
                c0 = (h * G_B + g) * HEAD_W
                o_ref[0, :, c0:c0 + HEAD_W] = og
            _attn_finish(store, acc[h], l_s[h], lam, sub_ref[...], tq, lam_init)


def _attn_sample_call(q, past_k, past_v, k_new, v_new, lam_params, sub_gain, *, tk, lam_init):
    B, tq, _ = q.shape
    past_len = past_k.shape[1]
    assert past_len % tk == 0 and past_len % CHUNK == 0 and tq <= CHUNK
    n_past = past_len // tk
    n = N_STACK * tq
    rows = lambda a: a.reshape(B, -1, HEAD_W)
    past_k, past_v, k_new, v_new = rows(past_k), rows(past_v), rows(k_new), rows(v_new)
    past_spec = pl.BlockSpec((1, tk * N_KV_B, HEAD_W), lambda b, j: (b, jnp.minimum(j, n_past - 1), 0))
    new_spec = pl.BlockSpec((1, tq * N_KV_B, HEAD_W), lambda b, j: (b, 0, 0))
    q_spec = pl.BlockSpec((1, tq, H_B * HEAD_W), lambda b, j: (b, 0, 0))
    small = [_const_spec((1, DH_B))] * 4 + [_const_spec((1, HEAD_W))]
    return pl.pallas_call(
        functools.partial(_attn_sample_kernel, tq=tq, n_past=n_past,
                          past_len=past_len, lam_init=lam_init),
        grid=(B, n_past + 1),
        in_specs=[q_spec, past_spec, past_spec, new_spec, new_spec] + small,
        out_specs=q_spec,
        out_shape=jax.ShapeDtypeStruct((B, tq, H_B * HEAD_W), BF16),
        scratch_shapes=[
            pltpu.VMEM((N_KV_B, HEAD_W, n), BF16),
            pltpu.VMEM((N_KV_B, HEAD_W, n), F32),
            pltpu.VMEM((N_KV_B, 1, n), F32),
            pltpu.VMEM((N_KV_B, 1, n), F32),
        ],
        compiler_params=pltpu.CompilerParams(
            dimension_semantics=("arbitrary",) * 2,
            vmem_limit_bytes=VMEM_LIMIT),
        name="attn_sample",
    )(q, past_k, past_v, k_new, v_new, *lam_params, sub_gain)


def _pick_tile(n, want):
    t = min(n, want)
    assert n % t == 0
    return t


def kernel(x_prompt, x_sample, state_hgrn, cache_k, cache_v, norm_mix_pre, norm_mix_post,
           norm_mlp_pre, norm_mlp_post, w_up, w_down, w_in_a, lb_logits, gnorm_a, w_out_a,
           norm_kv, w_kv, w_q_b, lam_q1, lam_k1, lam_q2, lam_k2, subln_b, w_out_b):
    row = lambda a: a.reshape(1, -1)
    w_in_bf = w_in_a[0].astype(BF16)
    w_out_a_bf = w_out_a[0].astype(BF16)
    w_up_bf = w_up.astype(BF16)
    w_down_bf = w_down.astype(BF16)
    w_kv_bf = w_kv.astype(BF16)
    w_q_bf = w_q_b[0].astype(BF16)
    w_out_b_bf = w_out_b[0].astype(BF16)
    lam_params = [row(lam_q1[0]), row(lam_k1[0]), row(lam_q2[0]), row(lam_k2[0])]
    lam_init = 0.8 - 0.6 * math.exp(-LAMBDA_INIT_SCALE * 1)

    def mlp(layer, x, a, w_o):
        return _mlp_call(x, a, w_o, row(norm_mix_post[layer]), row(norm_mlp_pre[layer]),
                         w_up_bf[layer], w_down_bf[layer], row(norm_mlp_post[layer]),
                         tile=_pick_tile(x.shape[0], 512))

    def trunk(x, s0, past_k, past_v):
        B, L, _ = x.shape
        n = B * L
        o_a, s_out = _hgrn_call(x, s0, row(norm_mix_pre[0]), w_in_bf, lb_logits, row(gnorm_a[0]),
                                tile=_pick_tile(L, 256))
        h = mlp(0, x.reshape(n, D_MODEL), o_a.reshape(n, D_A), w_out_a_bf)

        pos0 = 0 if past_k is None else past_k.shape[1]
        tile = _pick_tile(n, 512)
        reps = max(1, tile // L)
        tables = _rope_tables(pos0 + jnp.arange(L, dtype=jnp.int32))
        tables = [jnp.tile(t, (reps, 1)) for t in tables]
        outs = _qkv_call(h, row(norm_kv), row(norm_mix_pre[1]), w_kv_bf, w_q_bf, tables,
                         batch=B, tile=tile, resident=past_k is None)
        k, v, q = outs[:3]
        k = k.reshape(B, L, N_KV_B, HEAD_W)
        v = v.reshape(B, L, N_KV_B, HEAD_W)
        q3 = q.reshape(B, L, H_B * HEAD_W)
        if past_k is None:
            o_b = _attn_prompt_call(q3, outs[3], outs[4], lam_params, row(subln_b[0]),
                                    tq=256, tk=512, lam_init=lam_init)
        else:
            o_b = _attn_sample_call(q3, past_k, past_v, k, v, lam_params, row(subln_b[0]),
                                    tk=512, lam_init=lam_init)
        y = mlp(1, h, o_b.reshape(n, D_MODEL), w_out_b_bf)
        return y.reshape(B, L, D_MODEL), s_out[None], k, v

    s0_prompt = jnp.zeros((x_prompt.shape[0], H_A, DK_A, DV_A), F32)
    y_p, s_p, k_p, v_p = trunk(x_prompt, s0_prompt, None, None)
    y_s, s_s, k_s, v_s = trunk(x_sample, state_hgrn[0], cache_k, cache_v)
    return (y_p, y_s, s_p, k_p, v_p, s_s, k_s, v_s)
```

```python
import functools
import math

import jax
import jax.numpy as jnp
from jax import lax
from jax.experimental import pallas as pl
from jax.experimental.pallas import tpu as pltpu

F32 = jnp.float32
BF16 = jnp.bfloat16

D_MODEL = 1024
CHUNK = 64
SUB = 8
N_SUB = CHUNK // SUB
H_A = 8
DK_A = 128
DV_A = 128
D_A = H_A * DK_A
H_B = 8
DH_B = 64
N_KV_B = 4
G_B = H_B // N_KV_B
HEAD_W = 2 * DH_B
ROT_DIM = DH_B // 4
ROT_HALF = ROT_DIM // 2
ROPE_THETA = 500000.0
LAMBDA_INIT_SCALE = 0.3
D_FF = 4 * D_MODEL
FF_BLOCK = 1024
EPS = 1e-6
NEG_BIG = -1e30
LANE = 128
LOG2_E = math.log2(math.e)

VMEM_LIMIT = 52 * 1024 * 1024


def _rms(x):
    return x * lax.rsqrt(jnp.mean(x * x, axis=-1, keepdims=True) + EPS)


def _sigmoid(x):
    return 1.0 / (1.0 + jnp.exp(-x))


def _dot(a, b):
    return jnp.dot(a, b, preferred_element_type=F32)


def _dot_nt(a, b):
    return lax.dot_general(a, b, (((1,), (1,)), ((), ())), preferred_element_type=F32)


def _const_spec(shape):
    nd = len(shape)
    return pl.BlockSpec(shape, lambda *_: (0,) * nd)


def _cumsum_rows(tri, g):
    hi = g.astype(BF16)
    r1 = g - hi.astype(F32)
    mid = r1.astype(BF16)
    lo = (r1 - mid.astype(F32)).astype(BF16)
    return _dot(tri, hi) + _dot(tri, mid) + _dot(tri, lo)


def _hgrn_consts():
    rows = lax.broadcasted_iota(jnp.int32, (CHUNK, CHUNK), 0)
    cols = lax.broadcasted_iota(jnp.int32, (CHUNK, CHUNK), 1)
    tri = (rows >= cols).astype(BF16)
    n_off = SUB * (N_SUB * (N_SUB - 1) // 2)
    r_blk = lax.broadcasted_iota(jnp.int32, (CHUNK, n_off), 0) // SUB
    c_idx = lax.broadcasted_iota(jnp.int32, (CHUNK, n_off), 1)
    c_grp = jnp.zeros((CHUNK, n_off), jnp.int32)
    start = 0
    for i in range(1, N_SUB):
        c_grp = jnp.where(c_idx >= start, i, c_grp)
        start += SUB * i
    causal = (lax.broadcasted_iota(jnp.int32, (CHUNK, LANE), 0)
              >= lax.broadcasted_iota(jnp.int32, (CHUNK, LANE), 1))
    return tri, r_blk == c_grp, causal


def _hgrn_chunk(q, k, g2, v, st, tri, own_group, causal, c_ref):
    heads = [slice(h * DK_A, (h + 1) * DK_A) for h in range(H_A)]
    b = _cumsum_rows(tri, g2)
    b_last = b[CHUNK - 1:CHUNK, :]
    c = b - jnp.log2(k)
    for h, hs in enumerate(heads):
        c_ref[h] = c[:, hs]

    refs = [b[SUB * i - 1:SUB * i, :] for i in range(1, N_SUB)]
    ref_rows = jnp.concatenate(
        [jnp.zeros((SUB, D_A), F32)] + [jnp.broadcast_to(r, (SUB, D_A)) for r in refs], axis=0)
    qe = (q * jnp.exp2(b)).astype(BF16)
    qd = (q * jnp.exp2(b - ref_rows)).astype(BF16)
    kd = jnp.concatenate(
        [jnp.exp2(refs[i - 1] - c[:SUB * i]) for i in range(1, N_SUB)], axis=0).astype(BF16)
    v_bf = v.astype(BF16)
    v_all = jnp.concatenate([v_bf[:SUB * i] for i in range(1, N_SUB)], axis=0)
    kdl = jnp.exp2(b_last - c).astype(BF16)
    keep = jnp.exp2(b_last)

    o = [_dot_nt(qe[:, hs], st[h].astype(BF16)) for h, hs in enumerate(heads)]
    sc = [_dot_nt(qd[:, hs], kd[:, hs]) for hs in heads]
    sc = [jnp.where(own_group, x, 0.0).astype(BF16) for x in sc]
    o = [o[h] + _dot(sc[h], v_all[:, hs]) for h, hs in enumerate(heads)]
    st_new = [st[h] * keep[:, hs] + _dot(v[:, hs].T.astype(BF16), kdl[:, hs])
              for h, hs in enumerate(heads)]

    lane = lax.broadcasted_iota(jnp.int32, (SUB, LANE), 1)
    a_rows = [[] for _ in heads]
    for i in range(N_SUB):
        sl = slice(SUB * i, SUB * (i + 1))
        q_r, b_r = q[sl], b[sl]
        a_i = [jnp.zeros((SUB, LANE), F32) for _ in heads]
        for s in range(SUB):
            for h, hs in enumerate(heads):
                c_s = c_ref[h, pl.ds(SUB * i + s, SUB, stride=0), :]
                w = q_r[:, hs] * jnp.exp2(b_r[:, hs] - c_s)
                col = jnp.sum(w, axis=-1, keepdims=True)
                a_i[h] = jnp.where(lane == SUB * i + s, col, a_i[h])
        for h in range(H_A):
            a_rows[h].append(a_i[h])
    v_pad = jnp.concatenate([v_bf, jnp.zeros((LANE - CHUNK, D_A), BF16)], axis=0)
    for h, hs in enumerate(heads):
        a = jnp.where(causal, jnp.concatenate(a_rows[h], axis=0), 0.0)
        o[h] = o[h] + _dot(a.astype(BF16), v_pad[:, hs])
    return o, st_new


def _hgrn_kernel(x_ref, s0_ref, gpre_ref, win_ref, lbl_ref, gg_ref,
                 o_ref, sout_ref,
                 q_s, k_s, g_s, v_s, gate_s, st_s, c_s, *, tile):
    t = pl.program_id(1)

    @pl.when(t == 0)
    def _():
        for h in range(H_A):
            st_s[h] = s0_ref[0, h].T

    lbl = lbl_ref[...]
    e = jnp.exp(lbl - jnp.max(lbl, axis=0, keepdims=True))
    lb = e[0:1, :] / jnp.sum(e, axis=0, keepdims=True)

    u = (_rms(x_ref[0]) * gpre_ref[...]).astype(BF16)
    qf = _dot(u, win_ref[:, 0 * D_A:1 * D_A])
    ff = _dot(u, win_ref[:, 1 * D_A:2 * D_A])
    vf = _dot(u, win_ref[:, 2 * D_A:3 * D_A])
    gf = _dot(u, win_ref[:, 3 * D_A:4 * D_A])
    forget = lb + (1.0 - lb) * _sigmoid(ff)
    q = qf * _sigmoid(qf) * (DK_A ** -0.5)
    q_s[...] = q
    k_s[...] = 1.0 - forget
    g_s[...] = jnp.log2(forget)
    v_s[...] = vf
    gate_s[...] = gf * _sigmoid(gf)

    gg = gg_ref[...]
    tri, own_group, causal = _hgrn_consts()

    def chunk_body(c, carry):
        rs = pl.ds(pl.multiple_of(c * CHUNK, CHUNK), CHUNK)
        o, st_new = _hgrn_chunk(q_s[rs, :], k_s[rs, :], g_s[rs, :], v_s[rs, :],
                                [st_s[h] for h in range(H_A)], tri, own_group, causal, c_s)
        for h in range(H_A):
            hs = slice(h * DV_A, (h + 1) * DV_A)
            st_s[h] = st_new[h]
            o_ref[0, rs, hs] = (_rms(o[h]) * gg * gate_s[rs, hs]).astype(BF16)
        return carry

    lax.fori_loop(0, tile // CHUNK, chunk_body, 0)

    @pl.when(t == pl.num_programs(1) - 1)
    def _():
        for h in range(H_A):
            sout_ref[0, h] = st_s[h].T


def _hgrn_call(x, s0, gpre, w_in, lb_logits, ggain, *, tile):
    B, L, _ = x.shape
    grid = (B, L // tile)
    row_scratch = pltpu.VMEM((tile, D_A), F32)
    return pl.pallas_call(
        functools.partial(_hgrn_kernel, tile=tile),
        grid=grid,
        in_specs=[
            pl.BlockSpec((1, tile, D_MODEL), lambda b, t: (b, t, 0)),
            pl.BlockSpec((1, H_A, DK_A, DV_A), lambda b, t: (b, 0, 0, 0)),
            _const_spec((1, D_MODEL)),
            _const_spec((D_MODEL, 4 * D_A)),
            _const_spec(lb_logits.shape),
            _const_spec((1, DV_A)),
        ],
        out_specs=[
            pl.BlockSpec((1, tile, D_A), lambda b, t: (b, t, 0)),
            pl.BlockSpec((1, H_A, DK_A, DV_A), lambda b, t: (b, 0, 0, 0)),
        ],
        out_shape=[
            jax.ShapeDtypeStruct((B, L, D_A), BF16),
            jax.ShapeDtypeStruct((B, H_A, DK_A, DV_A), F32),
        ],
        scratch_shapes=[row_scratch] * 5 + [pltpu.VMEM((H_A, DV_A, DK_A), F32),
                                            pltpu.VMEM((H_A, CHUNK, DK_A), F32)],
        compiler_params=pltpu.CompilerParams(
            dimension_semantics=("arbitrary", "arbitrary"),
            vmem_limit_bytes=VMEM_LIMIT),
        name="hgrn",
    )(x, s0, gpre, w_in, lb_logits, ggain)


def _mlp_kernel(x_ref, a_ref, wo_ref, gpost_ref, gpre_ref, wup_ref, wdn_ref, gmpost_ref, y_ref):
    h = x_ref[...] + _rms(_dot(a_ref[...], wo_ref[...])) * gpost_ref[...]
    hn = (_rms(h) * gpre_ref[...]).astype(BF16)
    acc = jnp.zeros(h.shape, F32)
    for c in range(D_FF // FF_BLOCK):
        sl = slice(c * FF_BLOCK, (c + 1) * FF_BLOCK)
        u = jnp.maximum(_dot(hn, wup_ref[:, sl]), 0.0)
        acc = acc + _dot((u * u).astype(BF16), wdn_ref[sl, :])
    y_ref[...] = h + _rms(acc) * gmpost_ref[...]


def _mlp_call(x, a, w_o, gpost, gpre, w_up, w_down, gmpost, *, tile):
    n = x.shape[0]
    row = lambda i: (i, 0)
    return pl.pallas_call(
        _mlp_kernel,
        grid=(n // tile,),
        in_specs=[
            pl.BlockSpec((tile, D_MODEL), row),
            pl.BlockSpec((tile, D_MODEL), row),
            _const_spec((D_MODEL, D_MODEL)),
            _const_spec((1, D_MODEL)),
            _const_spec((1, D_MODEL)),
            _const_spec((D_MODEL, D_FF)),
            _const_spec((D_FF, D_MODEL)),
            _const_spec((1, D_MODEL)),
        ],
        out_specs=pl.BlockSpec((tile, D_MODEL), row),
        out_shape=jax.ShapeDtypeStruct((n, D_MODEL), F32),
        compiler_params=pltpu.CompilerParams(
            dimension_semantics=("arbitrary",),
            vmem_limit_bytes=VMEM_LIMIT),
        name="mlp",
    )(x, a, w_o, gpost, gpre, w_up, w_down, gmpost)


def _rope_tables(pos):
    inv_freq = ROPE_THETA ** (-jnp.arange(0, ROT_DIM, 2, dtype=F32) / ROT_DIM)
    ang = pos.astype(F32)[:, None] * inv_freq[None, :]
    cos, sin = jnp.cos(ang), jnp.sin(ang)
    n = pos.shape[0]
    rest = DH_B - ROT_DIM
    one = jnp.ones((n, rest), F32)
    zero = jnp.zeros((n, rest), F32)
    zh = jnp.zeros((n, ROT_HALF), F32)
    c64 = jnp.concatenate([cos, cos, one], axis=1)
    up64 = jnp.concatenate([-sin, zh, zero], axis=1)
    dn64 = jnp.concatenate([zh, sin, zero], axis=1)
    twice = lambda t: jnp.concatenate([t, t], axis=1)
    return twice(c64), twice(up64), twice(dn64)


def _qkv_kernel(h_ref, gkv_ref, gq_ref, wkv_ref, wq_ref, cos_ref, up_ref, dn_ref,
                k_ref, v_ref, q_ref, *resident):
    tile = h_ref.shape[0]
    xh = _rms(h_ref[...])
    kv = _dot((xh * gkv_ref[...]).astype(BF16), wkv_ref[...])
    q = _dot((xh * gq_ref[...]).astype(BF16), wq_ref[...])
    cos, up, dn = cos_ref[...], up_ref[...], dn_ref[...]

    def rope(x):
        return (x * cos + pltpu.roll(x, LANE - ROT_HALF, axis=1) * up
                + pltpu.roll(x, ROT_HALF, axis=1) * dn)

    n_k = N_KV_B * HEAD_W
    for j in range(N_KV_B):
        kj = rope(kv[:, j * HEAD_W:(j + 1) * HEAD_W])
        vj = kv[:, n_k + j * HEAD_W:n_k + (j + 1) * HEAD_W]
        k_ref[pl.ds(j, tile, stride=N_KV_B), :] = kj
        v_ref[pl.ds(j, tile, stride=N_KV_B), :] = vj
        if resident:
            kb_ref, vt_ref = resident
            kb_ref[0, j] = kj.astype(BF16)
            vt_ref[0, j] = vj.T.astype(BF16)
    for j in range(H_B):
        sl = slice(j * HEAD_W, (j + 1) * HEAD_W)
        q_ref[:, sl] = (rope(q[:, sl]) * (DH_B ** -0.5 * LOG2_E)).astype(BF16)


def _qkv_call(h, gkv, gq, w_kv, w_q, tables, *, batch, tile, resident):
    n = h.shape[0]
    seq = n // batch
    per_seq = max(1, seq // tile)
    n_tab = tables[0].shape[0] // tile
    row = lambda i: (i, 0)
    tab = pl.BlockSpec((tile, LANE), lambda i: (i % n_tab, 0))
    n_k = N_KV_B * HEAD_W
    assert seq % tile == 0 or not resident
    kv_spec = pl.BlockSpec((tile * N_KV_B, HEAD_W), row)
    out_specs = [kv_spec, kv_spec, pl.BlockSpec((tile, H_B * HEAD_W), row)]
    out_shape = [
        jax.ShapeDtypeStruct((n * N_KV_B, HEAD_W), F32),
        jax.ShapeDtypeStruct((n * N_KV_B, HEAD_W), F32),
        jax.ShapeDtypeStruct((n, H_B * HEAD_W), BF16),
    ]
    if resident:
        out_specs += [
            pl.BlockSpec((1, N_KV_B, tile, HEAD_W), lambda i: (i // per_seq, 0, i % per_seq, 0)),
            pl.BlockSpec((1, N_KV_B, HEAD_W, tile), lambda i: (i // per_seq, 0, 0, i % per_seq)),
        ]
        out_shape += [
            jax.ShapeDtypeStruct((batch, N_KV_B, seq, HEAD_W), BF16),
            jax.ShapeDtypeStruct((batch, N_KV_B, HEAD_W, seq), BF16),
        ]
    return pl.pallas_call(
        _qkv_kernel,
        grid=(n // tile,),
        in_specs=[
            pl.BlockSpec((tile, D_MODEL), row),
            _const_spec((1, D_MODEL)),
            _const_spec((1, D_MODEL)),
            _const_spec((D_MODEL, 2 * n_k)),
            _const_spec((D_MODEL, H_B * HEAD_W)),
            tab, tab, tab,
        ],
        out_specs=out_specs,
        out_shape=out_shape,
        compiler_params=pltpu.CompilerParams(
            dimension_semantics=("arbitrary",),
            vmem_limit_bytes=VMEM_LIMIT),
        name="qkv",
    )(h, gkv, gq, w_kv, w_q, *tables)


N_STACK = 2 * G_B
COL_GROUP = 256


def _stacked_q_t(q, tq):
    lane = lax.broadcasted_iota(jnp.int32, (tq, HEAD_W), 1)
    parts = []
    for g in range(G_B):
        qg = q[:, g * HEAD_W:(g + 1) * HEAD_W].astype(F32)
        parts.append(jnp.where(lane < DH_B, qg, 0.0))
        parts.append(jnp.where(lane >= DH_B, qg, 0.0))
    return jnp.concatenate(parts, axis=0).T.astype(BF16)


def _attn_consume(s, vt_blk, acc, m_prev, l_prev, mask):
    if mask is not None:
        s = jnp.where(mask, s, NEG_BIG)
    m_new = jnp.maximum(m_prev, jnp.max(s, axis=0, keepdims=True))
    alpha = jnp.exp2(m_prev - m_new)
    p = jnp.exp2(s - m_new)
    l_new = alpha * l_prev + jnp.sum(p, axis=0, keepdims=True)
    p = p.astype(BF16)
    n = s.shape[1]
    for c in range(n // COL_GROUP):
        cs = slice(c * COL_GROUP, (c + 1) * COL_GROUP)
        acc[:, cs] = acc[:, cs] * alpha[:, cs] + _dot(vt_blk, p[:, cs])
    return m_new, l_new


def _chunk_mask(q_pos0, k_pos0, tq, tk):
    c = lax.broadcasted_iota(jnp.int32, (1, N_STACK * tq), 1)
    qc = (q_pos0 + c % tq) // CHUNK
    kc = (k_pos0 + lax.broadcasted_iota(jnp.int32, (tk, 1), 0)) // CHUNK
    return kc <= qc


def _lambda(lq1, lk1, lq2, lk2, lam_init):
    return (jnp.exp(jnp.sum(lq1[...] * lk1[...], axis=1, keepdims=True))
            - jnp.exp(jnp.sum(lq2[...] * lk2[...], axis=1, keepdims=True)) + lam_init)


def _attn_finish(store, acc, l, lam, sub, tq, lam_init):
    o = (acc * (1.0 / l)).T
    for g in range(G_B):
        og = o[(2 * g) * tq:(2 * g + 1) * tq] - lam * o[(2 * g + 1) * tq:(2 * g + 2) * tq]
        store(g, (_rms(og) * sub * (1.0 - lam_init)).astype(BF16))


def _attn_prompt_kernel(q_ref, kb_ref, vt_ref, lq1, lk1, lq2, lk2, sub_ref, o_ref,
                        qst, acc, s_a, s_b, ml_s, *, tq, tk, lam_init):
    i = pl.program_id(2)
    n = N_STACK * tq
    qst[...] = _stacked_q_t(q_ref[0], tq)
    acc[...] = jnp.zeros(acc.shape, F32)

    n_full = (i * tq + CHUNK) // tk
    n_need = ((i + 1) * tq + tk - 1) // tk

    def scores(j):
        r0 = pl.multiple_of(j * tk, tk)
        return _dot(kb_ref[0, 0, pl.ds(r0, tk), :], qst[...])

    def consume(buf, j, carry, masked):
        r0 = pl.multiple_of(j * tk, tk)
        mask = _chunk_mask(i * tq, r0, tq, tk) if masked else None
        return _attn_consume(buf[...], vt_ref[0, 0, :, pl.ds(r0, tk)], acc,
                             carry[0], carry[1], mask)

    def pair(jj, carry):
        j = 2 * jj
        s_b[...] = scores(j + 1)
        carry = consume(s_a, j, carry, False)
        s_a[...] = scores(j + 2)
        return consume(s_b, j + 1, carry, False)

    s_a[...] = scores(0)
    carry = (jnp.full((1, n), NEG_BIG, F32), jnp.zeros((1, n), F32))
    carry = lax.fori_loop(0, n_full // 2, pair, carry)
    j_tail = 2 * (n_full // 2)

    @pl.when(n_full % 2 == 1)
    def _():
        s_b[...] = scores(j_tail + 1)
        m, l = consume(s_a, j_tail, carry, False)
        m, l = consume(s_b, j_tail + 1, (m, l), True)
        ml_s[0:1, :] = m
        ml_s[1:2, :] = l

    @pl.when(n_full % 2 == 0)
    def _():
        m, l = consume(s_a, j_tail, carry, True)
        ml_s[0:1, :] = m
        ml_s[1:2, :] = l

    l = ml_s[1:2, :]

    def store(g, og):
        o_ref[0, :, g * HEAD_W:(g + 1) * HEAD_W] = og

    _attn_finish(store, acc[...], l, _lambda(lq1, lk1, lq2, lk2, lam_init), sub_ref[...],
                 tq, lam_init)


def _attn_prompt_call(q, kb, vt, lam_params, sub_gain, *, tq, tk, lam_init):
    B, L, _ = q.shape
    assert L % tq == 0 and L % tk == 0 and tq % CHUNK == 0
    assert all(((i + 1) * tq + tk - 1) // tk - (i * tq + CHUNK) // tk == 1 for i in range(L // tq))
    n = N_STACK * tq
    q_spec = pl.BlockSpec((1, tq, G_B * HEAD_W), lambda b, h, i: (b, i, h))
    small = [_const_spec((1, DH_B))] * 4 + [_const_spec((1, HEAD_W))]
    return pl.pallas_call(
        functools.partial(_attn_prompt_kernel, tq=tq, tk=tk, lam_init=lam_init),
        grid=(B, N_KV_B, L // tq),
        in_specs=[
            q_spec,
            pl.BlockSpec((1, 1, L, HEAD_W), lambda b, h, i: (b, h, 0, 0)),
            pl.BlockSpec((1, 1, HEAD_W, L), lambda b, h, i: (b, h, 0, 0)),
        ] + small,
        out_specs=q_spec,
        out_shape=jax.ShapeDtypeStruct((B, L, H_B * HEAD_W), BF16),
        scratch_shapes=[pltpu.VMEM((HEAD_W, n), BF16), pltpu.VMEM((HEAD_W, n), F32),
                        pltpu.VMEM((tk, n), F32), pltpu.VMEM((tk, n), F32),
                        pltpu.VMEM((2, n), F32)],
        compiler_params=pltpu.CompilerParams(
            dimension_semantics=("arbitrary",) * 3,
            vmem_limit_bytes=VMEM_LIMIT),
        name="attn_prompt",
    )(q, kb, vt, *lam_params, sub_gain)


def _attn_sample_kernel(q_ref, pk_ref, pv_ref, nk_ref, nv_ref, lq1, lk1, lq2, lk2, sub_ref, o_ref,
                        qst, acc, m_s, l_s, *, tq, n_past, past_len, lam_init):
    j = pl.program_id(1)

    @pl.when(j == 0)
    def _():
        q = q_ref[0]
        for h in range(N_KV_B):
            qst[h] = _stacked_q_t(q[:, h * G_B * HEAD_W:(h + 1) * G_B * HEAD_W], tq)
        acc[...] = jnp.zeros(acc.shape, F32)
        m_s[...] = jnp.full(m_s.shape, NEG_BIG, F32)
        l_s[...] = jnp.zeros(l_s.shape, F32)

    def run(k_ref, v_ref, mask):
        keys = k_ref.shape[1] // N_KV_B
        head = lambda ref, h: ref[0, pl.ds(h, keys, stride=N_KV_B), :]
        ss = [_dot(head(k_ref, h).astype(BF16), qst[h]) for h in range(N_KV_B)]
        vts = [head(v_ref, h).T.astype(BF16) for h in range(N_KV_B)]
        for h in range(N_KV_B):
            m_new, l_new = _attn_consume(ss[h], vts[h], acc.at[h], m_s[h], l_s[h], mask)
            m_s[h] = m_new
            l_s[h] = l_new

    @pl.when(j < n_past)
    def _():
        run(pk_ref, pv_ref, None)

    @pl.when(j == n_past)
    def _():
        run(nk_ref, nv_ref, _chunk_mask(past_len, past_len, tq, tq))
        lam = _lambda(lq1, lk1, lq2, lk2, lam_init)
        for h in range(N_KV_B):
            def store(g, og, h=h):
                c0 = (h * G_B + g) * HEAD_W
                o_ref[0, :, c0:c0 + HEAD_W] = og
            _attn_finish(store, acc[h], l_s[h], lam, sub_ref[...], tq, lam_init)


def _attn_sample_call(q, past_k, past_v, k_new, v_new, lam_params, sub_gain, *, tk, lam_init):
    B, tq, _ = q.shape
    past_len = past_k.shape[1]
    assert past_len % tk == 0 and past_len % CHUNK == 0 and tq <= CHUNK
    n_past = past_len // tk
    n = N_STACK * tq
    rows = lambda a: a.reshape(B, -1, HEAD_W)
    past_k, past_v, k_new, v_new = rows(past_k), rows(past_v), rows(k_new), rows(v_new)
    past_spec = pl.BlockSpec((1, tk * N_KV_B, HEAD_W), lambda b, j: (b, jnp.minimum(j, n_past - 1), 0))
    new_spec = pl.BlockSpec((1, tq * N_KV_B, HEAD_W), lambda b, j: (b, 0, 0))
    q_spec = pl.BlockSpec((1, tq, H_B * HEAD_W), lambda b, j: (b, 0, 0))
    small = [_const_spec((1, DH_B))] * 4 + [_const_spec((1, HEAD_W))]
    return pl.pallas_call(
        functools.partial(_attn_sample_kernel, tq=tq, n_past=n_past,
                          past_len=past_len, lam_init=lam_init),
        grid=(B, n_past + 1),
        in_specs=[q_spec, past_spec, past_spec, new_spec, new_spec] + small,
        out_specs=q_spec,
        out_shape=jax.ShapeDtypeStruct((B, tq, H_B * HEAD_W), BF16),
        scratch_shapes=[
            pltpu.VMEM((N_KV_B, HEAD_W, n), BF16),
            pltpu.VMEM((N_KV_B, HEAD_W, n), F32),
            pltpu.VMEM((N_KV_B, 1, n), F32),
            pltpu.VMEM((N_KV_B, 1, n), F32),
        ],
        compiler_params=pltpu.CompilerParams(
            dimension_semantics=("arbitrary",) * 2,
            vmem_limit_bytes=VMEM_LIMIT),
        name="attn_sample",
    )(q, past_k, past_v, k_new, v_new, *lam_params, sub_gain)


def _pick_tile(n, want):
    t = min(n, want)
    assert n % t == 0
    return t


def kernel(x_prompt, x_sample, state_hgrn, cache_k, cache_v, norm_mix_pre, norm_mix_post,
           norm_mlp_pre, norm_mlp_post, w_up, w_down, w_in_a, lb_logits, gnorm_a, w_out_a,
           norm_kv, w_kv, w_q_b, lam_q1, lam_k1, lam_q2, lam_k2, subln_b, w_out_b):
    row = lambda a: a.reshape(1, -1)
    w_in_bf = w_in_a[0].astype(BF16)
    w_out_a_bf = w_out_a[0].astype(BF16)
    w_up_bf = w_up.astype(BF16)
    w_down_bf = w_down.astype(BF16)
    w_kv_bf = w_kv.astype(BF16)
    w_q_bf = w_q_b[0].astype(BF16)
    w_out_b_bf = w_out_b[0].astype(BF16)
    lam_params = [row(lam_q1[0]), row(lam_k1[0]), row(lam_q2[0]), row(lam_k2[0])]
    lam_init = 0.8 - 0.6 * math.exp(-LAMBDA_INIT_SCALE * 1)

    def mlp(layer, x, a, w_o):
        return _mlp_call(x, a, w_o, row(norm_mix_post[layer]), row(norm_mlp_pre[layer]),
                         w_up_bf[layer], w_down_bf[layer], row(norm_mlp_post[layer]),
                         tile=_pick_tile(x.shape[0], 512))

    def trunk(x, s0, past_k, past_v):
        B, L, _ = x.shape
        n = B * L
        o_a, s_out = _hgrn_call(x, s0, row(norm_mix_pre[0]), w_in_bf, lb_logits, row(gnorm_a[0]),
                                tile=_pick_tile(L, 256))
        h = mlp(0, x.reshape(n, D_MODEL), o_a.reshape(n, D_A), w_out_a_bf)

        pos0 = 0 if past_k is None else past_k.shape[1]
        tile = _pick_tile(n, 512)
        reps = max(1, tile // L)
        tables = _rope_tables(pos0 + jnp.arange(L, dtype=jnp.int32))
        tables = [jnp.tile(t, (reps, 1)) for t in tables]
        outs = _qkv_call(h, row(norm_kv), row(norm_mix_pre[1]), w_kv_bf, w_q_bf, tables,
                         batch=B, tile=tile, resident=past_k is None)
        k, v, q = outs[:3]
        k = k.reshape(B, L, N_KV_B, HEAD_W)
        v = v.reshape(B, L, N_KV_B, HEAD_W)
        q3 = q.reshape(B, L, H_B * HEAD_W)
        if past_k is None:
            o_b = _attn_prompt_call(q3, outs[3], outs[4], lam_params, row(subln_b[0]),
                                    tq=512, tk=512, lam_init=lam_init)
        else:
            o_b = _attn_sample_call(q3, past_k, past_v, k, v, lam_params, row(subln_b[0]),
                                    tk=512, lam_init=lam_init)
        y = mlp(1, h, o_b.reshape(n, D_MODEL), w_out_b_bf)
        return y.reshape(B, L, D_MODEL), s_out[None], k, v

    s0_prompt = jnp.zeros((x_prompt.shape[0], H_A, DK_A, DV_A), F32)
    y_p, s_p, k_p, v_p = trunk(x_prompt, s0_prompt, None, None)
    y_s, s_s, k_s, v_s = trunk(x_sample, state_hgrn[0], cache_k, cache_v)
    return (y_p, y_s, s_p, k_p, v_p, s_s, k_s, v_s)
```

```python
import functools
import math

import jax
import jax.numpy as jnp
from jax import lax
from jax.experimental import pallas as pl
from jax.experimental.pallas import tpu as pltpu

F32 = jnp.float32
BF16 = jnp.bfloat16

D_MODEL = 1024
CHUNK = 64
SUB = 8
N_SUB = CHUNK // SUB
H_A = 8
DK_A = 128
DV_A = 128
D_A = H_A * DK_A
H_B = 8
DH_B = 64
N_KV_B = 4
G_B = H_B // N_KV_B
HEAD_W = 2 * DH_B
ROT_DIM = DH_B // 4
ROT_HALF = ROT_DIM // 2
ROPE_THETA = 500000.0
LAMBDA_INIT_SCALE = 0.3
D_FF = 4 * D_MODEL
FF_BLOCK = 1024
EPS = 1e-6
NEG_BIG = -1e30
LANE = 128
LOG2_E = math.log2(math.e)

VMEM_LIMIT = 52 * 1024 * 1024


def _rms(x):
    return x * lax.rsqrt(jnp.mean(x * x, axis=-1, keepdims=True) + EPS)


def _sigmoid(x):
    return 1.0 / (1.0 + jnp.exp(-x))


def _dot(a, b):
    return jnp.dot(a, b, preferred_element_type=F32)


def _dot_nt(a, b):
    return lax.dot_general(a, b, (((1,), (1,)), ((), ())), preferred_element_type=F32)


def _const_spec(shape):
    nd = len(shape)
    return pl.BlockSpec(shape, lambda *_: (0,) * nd)


def _cumsum_rows(tri, g):
    hi = g.astype(BF16)
    r1 = g - hi.astype(F32)
    mid = r1.astype(BF16)
    lo = (r1 - mid.astype(F32)).astype(BF16)
    return _dot(tri, hi) + _dot(tri, mid) + _dot(tri, lo)


def _hgrn_consts():
    rows = lax.broadcasted_iota(jnp.int32, (CHUNK, CHUNK), 0)
    cols = lax.broadcasted_iota(jnp.int32, (CHUNK, CHUNK), 1)
    tri = (rows >= cols).astype(BF16)
    n_off = SUB * (N_SUB * (N_SUB - 1) // 2)
    r_blk = lax.broadcasted_iota(jnp.int32, (CHUNK, n_off), 0) // SUB
    c_idx = lax.broadcasted_iota(jnp.int32, (CHUNK, n_off), 1)
    c_grp = jnp.zeros((CHUNK, n_off), jnp.int32)
    start = 0
    for i in range(1, N_SUB):
        c_grp = jnp.where(c_idx >= start, i, c_grp)
        start += SUB * i
    causal = (lax.broadcasted_iota(jnp.int32, (CHUNK, LANE), 0)
              >= lax.broadcasted_iota(jnp.int32, (CHUNK, LANE), 1))
    return tri, r_blk == c_grp, causal


def _hgrn_chunk(q, k, g2, v, st, tri, own_group, causal, c_ref):
    heads = [slice(h * DK_A, (h + 1) * DK_A) for h in range(H_A)]
    b = _cumsum_rows(tri, g2)
    b_last = b[CHUNK - 1:CHUNK, :]
    c = b - jnp.log2(k)
    for h, hs in enumerate(heads):
        c_ref[h] = c[:, hs]

    refs = [b[SUB * i - 1:SUB * i, :] for i in range(1, N_SUB)]
    ref_rows = jnp.concatenate(
        [jnp.zeros((SUB, D_A), F32)] + [jnp.broadcast_to(r, (SUB, D_A)) for r in refs], axis=0)
    qe = (q * jnp.exp2(b)).astype(BF16)
    qd = (q * jnp.exp2(b - ref_rows)).astype(BF16)
    kd = jnp.concatenate(
        [jnp.exp2(refs[i - 1] - c[:SUB * i]) for i in range(1, N_SUB)], axis=0).astype(BF16)
    v_bf = v.astype(BF16)
    v_all = jnp.concatenate([v_bf[:SUB * i] for i in range(1, N_SUB)], axis=0)
    kdl = jnp.exp2(b_last - c).astype(BF16)
    keep = jnp.exp2(b_last)

    o = [_dot_nt(qe[:, hs], st[h].astype(BF16)) for h, hs in enumerate(heads)]
    sc = [_dot_nt(qd[:, hs], kd[:, hs]) for hs in heads]
    sc = [jnp.where(own_group, x, 0.0).astype(BF16) for x in sc]
    o = [o[h] + _dot(sc[h], v_all[:, hs]) for h, hs in enumerate(heads)]
    st_new = [st[h] * keep[:, hs] + _dot(v[:, hs].T.astype(BF16), kdl[:, hs])
              for h, hs in enumerate(heads)]

    lane = lax.broadcasted_iota(jnp.int32, (SUB, LANE), 1)
    a_rows = [[] for _ in heads]
    for i in range(N_SUB):
        sl = slice(SUB * i, SUB * (i + 1))
        q_r, b_r = q[sl], b[sl]
        a_i = [jnp.zeros((SUB, LANE), F32) for _ in heads]
        for s in range(SUB):
            for h, hs in enumerate(heads):
                c_s = c_ref[h, pl.ds(SUB * i + s, SUB, stride=0), :]
                w = q_r[:, hs] * jnp.exp2(b_r[:, hs] - c_s)
                col = jnp.sum(w, axis=-1, keepdims=True)
                a_i[h] = jnp.where(lane == SUB * i + s, col, a_i[h])
        for h in range(H_A):
            a_rows[h].append(a_i[h])
    v_pad = jnp.concatenate([v_bf, jnp.zeros((LANE - CHUNK, D_A), BF16)], axis=0)
    for h, hs in enumerate(heads):
        a = jnp.where(causal, jnp.concatenate(a_rows[h], axis=0), 0.0)
        o[h] = o[h] + _dot(a.astype(BF16), v_pad[:, hs])
    return o, st_new


def _hgrn_kernel(x_ref, s0_ref, gpre_ref, win_ref, lbl_ref, gg_ref,
                 o_ref, sout_ref,
                 q_s, k_s, g_s, v_s, gate_s, st_s, c_s, *, tile):
    t = pl.program_id(1)

    @pl.when(t == 0)
    def _():
        for h in range(H_A):
            st_s[h] = s0_ref[0, h].T

    lbl = lbl_ref[...]
    e = jnp.exp(lbl - jnp.max(lbl, axis=0, keepdims=True))
    lb = e[0:1, :] / jnp.sum(e, axis=0, keepdims=True)

    u = (_rms(x_ref[0]) * gpre_ref[...]).astype(BF16)
    qf = _dot(u, win_ref[:, 0 * D_A:1 * D_A])
    ff = _dot(u, win_ref[:, 1 * D_A:2 * D_A])
    vf = _dot(u, win_ref[:, 2 * D_A:3 * D_A])
    gf = _dot(u, win_ref[:, 3 * D_A:4 * D_A])
    forget = lb + (1.0 - lb) * _sigmoid(ff)
    q = qf * _sigmoid(qf) * (DK_A ** -0.5)
    q_s[...] = q
    k_s[...] = 1.0 - forget
    g_s[...] = jnp.log2(forget)
    v_s[...] = vf
    gate_s[...] = gf * _sigmoid(gf)

    gg = gg_ref[...]
    tri, own_group, causal = _hgrn_consts()

    def chunk_body(c, carry):
        rs = pl.ds(pl.multiple_of(c * CHUNK, CHUNK), CHUNK)
        o, st_new = _hgrn_chunk(q_s[rs, :], k_s[rs, :], g_s[rs, :], v_s[rs, :],
                                [st_s[h] for h in range(H_A)], tri, own_group, causal, c_s)
        for h in range(H_A):
            hs = slice(h * DV_A, (h + 1) * DV_A)
            st_s[h] = st_new[h]
            o_ref[0, rs, hs] = (_rms(o[h]) * gg * gate_s[rs, hs]).astype(BF16)
        return carry

    lax.fori_loop(0, tile // CHUNK, chunk_body, 0)

    @pl.when(t == pl.num_programs(1) - 1)
    def _():
        for h in range(H_A):
            sout_ref[0, h] = st_s[h].T


def _hgrn_call(x, s0, gpre, w_in, lb_logits, ggain, *, tile):
    B, L, _ = x.shape
    grid = (B, L // tile)
    row_scratch = pltpu.VMEM((tile, D_A), F32)
    return pl.pallas_call(
        functools.partial(_hgrn_kernel, tile=tile),
        grid=grid,
        in_specs=[
            pl.BlockSpec((1, tile, D_MODEL), lambda b, t: (b, t, 0)),
            pl.BlockSpec((1, H_A, DK_A, DV_A), lambda b, t: (b, 0, 0, 0)),
            _const_spec((1, D_MODEL)),
            _const_spec((D_MODEL, 4 * D_A)),
            _const_spec(lb_logits.shape),
            _const_spec((1, DV_A)),
        ],
        out_specs=[
            pl.BlockSpec((1, tile, D_A), lambda b, t: (b, t, 0)),
            pl.BlockSpec((1, H_A, DK_A, DV_A), lambda b, t: (b, 0, 0, 0)),
        ],
        out_shape=[
            jax.ShapeDtypeStruct((B, L, D_A), BF16),
            jax.ShapeDtypeStruct((B, H_A, DK_A, DV_A), F32),
        ],
        scratch_shapes=[row_scratch] * 5 + [pltpu.VMEM((H_A, DV_A, DK_A), F32),
                                            pltpu.VMEM((H_A, CHUNK, DK_A), F32)],
        compiler_params=pltpu.CompilerParams(
            dimension_semantics=("arbitrary", "arbitrary"),
            vmem_limit_bytes=VMEM_LIMIT),
        name="hgrn",
    )(x, s0, gpre, w_in, lb_logits, ggain)


def _mlp_kernel(x_ref, a_ref, wo_ref, gpost_ref, gpre_ref, wup_ref, wdn_ref, gmpost_ref, y_ref):
    h = x_ref[...] + _rms(_dot(a_ref[...], wo_ref[...])) * gpost_ref[...]
    hn = (_rms(h) * gpre_ref[...]).astype(BF16)
    acc = jnp.zeros(h.shape, F32)
    for c in range(D_FF // FF_BLOCK):
        sl = slice(c * FF_BLOCK, (c + 1) * FF_BLOCK)
        u = jnp.maximum(_dot(hn, wup_ref[:, sl]), 0.0)
        acc = acc + _dot((u * u).astype(BF16), wdn_ref[sl, :])
    y_ref[...] = h + _rms(acc) * gmpost_ref[...]


def _mlp_call(x, a, w_o, gpost, gpre, w_up, w_down, gmpost, *, tile):
    n = x.shape[0]
    row = lambda i: (i, 0)
    return pl.pallas_call(
        _mlp_kernel,
        grid=(n // tile,),
        in_specs=[
            pl.BlockSpec((tile, D_MODEL), row),
            pl.BlockSpec((tile, D_MODEL), row),
            _const_spec((D_MODEL, D_MODEL)),
            _const_spec((1, D_MODEL)),
            _const_spec((1, D_MODEL)),
            _const_spec((D_MODEL, D_FF)),
            _const_spec((D_FF, D_MODEL)),
            _const_spec((1, D_MODEL)),
        ],
        out_specs=pl.BlockSpec((tile, D_MODEL), row),
        out_shape=jax.ShapeDtypeStruct((n, D_MODEL), F32),
        compiler_params=pltpu.CompilerParams(
            dimension_semantics=("arbitrary",),
            vmem_limit_bytes=VMEM_LIMIT),
        name="mlp",
    )(x, a, w_o, gpost, gpre, w_up, w_down, gmpost)


def _rope_tables(pos):
    lane = jnp.arange(LANE, dtype=jnp.int32) % DH_B
    pair = (2 * (lane % ROT_HALF)).astype(F32)
    inv_freq = jnp.where(lane < ROT_DIM, ROPE_THETA ** (-pair / ROT_DIM), 0.0)
    ang = pos.astype(F32)[:, None] * inv_freq[None, :]
    cos, sin = jnp.cos(ang), jnp.sin(ang)
    up = jnp.where(lane < ROT_HALF, -sin, 0.0)
    dn = jnp.where((lane >= ROT_HALF) & (lane < ROT_DIM), sin, 0.0)
    return cos, up, dn


def _qkv_kernel(h_ref, gkv_ref, gq_ref, wkv_ref, wq_ref, cos_ref, up_ref, dn_ref,
                k_ref, v_ref, q_ref, *resident):
    tile = h_ref.shape[0]
    xh = _rms(h_ref[...])
    kv = _dot((xh * gkv_ref[...]).astype(BF16), wkv_ref[...])
    q = _dot((xh * gq_ref[...]).astype(BF16), wq_ref[...])
    cos, up, dn = cos_ref[...], up_ref[...], dn_ref[...]

    def rope(x):
        return (x * cos + pltpu.roll(x, LANE - ROT_HALF, axis=1) * up
                + pltpu.roll(x, ROT_HALF, axis=1) * dn)

    n_k = N_KV_B * HEAD_W
    for j in range(N_KV_B):
        kj = rope(kv[:, j * HEAD_W:(j + 1) * HEAD_W])
        vj = kv[:, n_k + j * HEAD_W:n_k + (j + 1) * HEAD_W]
        k_ref[pl.ds(j, tile, stride=N_KV_B), :] = kj
        v_ref[pl.ds(j, tile, stride=N_KV_B), :] = vj
        if resident:
            kb_ref, vt_ref = resident
            kb_ref[0, j] = kj.astype(BF16)
            vt_ref[0, j] = vj.T.astype(BF16)
    for j in range(H_B):
        sl = slice(j * HEAD_W, (j + 1) * HEAD_W)
        q_ref[:, sl] = (rope(q[:, sl]) * (DH_B ** -0.5 * LOG2_E)).astype(BF16)


def _qkv_call(h, gkv, gq, w_kv, w_q, tables, *, batch, tile, resident):
    n = h.shape[0]
    seq = n // batch
    per_seq = max(1, seq // tile)
    n_tab = tables[0].shape[0] // tile
    row = lambda i: (i, 0)
    tab = pl.BlockSpec((tile, LANE), lambda i: (i % n_tab, 0))
    n_k = N_KV_B * HEAD_W
    assert seq % tile == 0 or not resident
    kv_spec = pl.BlockSpec((tile * N_KV_B, HEAD_W), row)
    out_specs = [kv_spec, kv_spec, pl.BlockSpec((tile, H_B * HEAD_W), row)]
    out_shape = [
        jax.ShapeDtypeStruct((n * N_KV_B, HEAD_W), F32),
        jax.ShapeDtypeStruct((n * N_KV_B, HEAD_W), F32),
        jax.ShapeDtypeStruct((n, H_B * HEAD_W), BF16),
    ]
    if resident:
        out_specs += [
            pl.BlockSpec((1, N_KV_B, tile, HEAD_W), lambda i: (i // per_seq, 0, i % per_seq, 0)),
            pl.BlockSpec((1, N_KV_B, HEAD_W, tile), lambda i: (i // per_seq, 0, 0, i % per_seq)),
        ]
        out_shape += [
            jax.ShapeDtypeStruct((batch, N_KV_B, seq, HEAD_W), BF16),
            jax.ShapeDtypeStruct((batch, N_KV_B, HEAD_W, seq), BF16),
        ]
    return pl.pallas_call(
        _qkv_kernel,
        grid=(n // tile,),
        in_specs=[
            pl.BlockSpec((tile, D_MODEL), row),
            _const_spec((1, D_MODEL)),
            _const_spec((1, D_MODEL)),
            _const_spec((D_MODEL, 2 * n_k)),
            _const_spec((D_MODEL, H_B * HEAD_W)),
            tab, tab, tab,
        ],
        out_specs=out_specs,
        out_shape=out_shape,
        compiler_params=pltpu.CompilerParams(
            dimension_semantics=("arbitrary",),
            vmem_limit_bytes=VMEM_LIMIT),
        name="qkv",
    )(h, gkv, gq, w_kv, w_q, *tables)


N_STACK = 2 * G_B
COL_GROUP = 256
PV_LAG = 1


def _stacked_q(q, tq):
    lane = lax.broadcasted_iota(jnp.int32, (tq, HEAD_W), 1)
    zero = jnp.zeros((tq, HEAD_W), BF16)
    parts = []
    for g in range(G_B):
        qg = q[:, g * HEAD_W:(g + 1) * HEAD_W]
        parts.append(jnp.where(lane < DH_B, qg, zero))
        parts.append(jnp.where(lane >= DH_B, qg, zero))
    return jnp.concatenate(parts, axis=0)


def _stacked_q_t(q, tq):
    lane = lax.broadcasted_iota(jnp.int32, (tq, HEAD_W), 1)
    parts = []
    for g in range(G_B):
        qg = q[:, g * HEAD_W:(g + 1) * HEAD_W].astype(F32)
        parts.append(jnp.where(lane < DH_B, qg, 0.0))
        parts.append(jnp.where(lane >= DH_B, qg, 0.0))
    return jnp.concatenate(parts, axis=0).T.astype(BF16)


def _attn_consume(s, vt_blk, acc, m_prev, l_prev, mask):
    if mask is not None:
        s = jnp.where(mask, s, NEG_BIG)
    m_new = jnp.maximum(m_prev, jnp.max(s, axis=0, keepdims=True))
    alpha = jnp.exp2(m_prev - m_new)
    p = jnp.exp2(s - m_new)
    l_new = alpha * l_prev + jnp.sum(p, axis=0, keepdims=True)
    p = p.astype(BF16)
    n = s.shape[1]
    for c in range(n // COL_GROUP):
        cs = slice(c * COL_GROUP, (c + 1) * COL_GROUP)
        acc[:, cs] = acc[:, cs] * alpha[:, cs] + _dot(vt_blk, p[:, cs])
    return m_new, l_new


def _attn_consume_produce(src, dst, k_next, qs, vt_blk, acc, m_prev, l_prev, mask):
    n = src.shape[1]
    n_grp = n // COL_GROUP
    cs = [slice(c * COL_GROUP, (c + 1) * COL_GROUP) for c in range(n_grp)]
    m_out, l_out, p, alpha = [], [], [], []
    for c in range(n_grp + PV_LAG):
        if c < n_grp:
            s = src[:, cs[c]]
            if mask is not None:
                s = jnp.where(mask[:, cs[c]], s, NEG_BIG)
            m_new = jnp.maximum(m_prev[:, cs[c]], jnp.max(s, axis=0, keepdims=True))
            alpha.append(jnp.exp2(m_prev[:, cs[c]] - m_new))
            e = jnp.exp2(s - m_new)
            l_out.append(alpha[c] * l_prev[:, cs[c]] + jnp.sum(e, axis=0, keepdims=True))
            m_out.append(m_new)
            p.append(e.astype(BF16))
            if k_next is not None:
                dst[:, cs[c]] = _dot_nt(k_next, qs[cs[c], :])
        d = c - PV_LAG
        if d >= 0:
            acc[:, cs[d]] = acc[:, cs[d]] * alpha[d] + _dot(vt_blk, p[d])
    return jnp.concatenate(m_out, axis=1), jnp.concatenate(l_out, axis=1)


def _chunk_mask(q_pos0, k_pos0, tq, tk):
    c = lax.broadcasted_iota(jnp.int32, (1, N_STACK * tq), 1)
    qc = (q_pos0 + c % tq) // CHUNK
    kc = (k_pos0 + lax.broadcasted_iota(jnp.int32, (tk, 1), 0)) // CHUNK
    return kc <= qc


def _lambda(lq1, lk1, lq2, lk2, lam_init):
    return (jnp.exp(jnp.sum(lq1[...] * lk1[...], axis=1, keepdims=True))
            - jnp.exp(jnp.sum(lq2[...] * lk2[...], axis=1, keepdims=True)) + lam_init)


def _attn_finish(store, acc, l, lam, sub, tq, lam_init):
    o = (acc * (1.0 / l)).T
    for g in range(G_B):
        og = o[(2 * g) * tq:(2 * g + 1) * tq] - lam * o[(2 * g + 1) * tq:(2 * g + 2) * tq]
        store(g, (_rms(og) * sub * (1.0 - lam_init)).astype(BF16))


def _attn_prompt_kernel(q_ref, kb_ref, vt_ref, lq1, lk1, lq2, lk2, sub_ref, o_ref,
                        qs, acc, s_a, s_b, ml_s, *, tq, tk, lam_init):
    i = pl.program_id(2)
    n = N_STACK * tq
    qs[...] = _stacked_q(q_ref[0], tq)
    acc[...] = jnp.zeros(acc.shape, F32)

    n_full = (i * tq + CHUNK) // tk

    def kblk(j):
        return kb_ref[0, 0, pl.ds(pl.multiple_of(j * tk, tk), tk), :]

    def vblk(j):
        return vt_ref[0, 0, :, pl.ds(pl.multiple_of(j * tk, tk), tk)]

    def pair(jj, carry):
        j = 2 * jj
        carry = _attn_consume_produce(s_a, s_b, kblk(j + 1), qs, vblk(j), acc, *carry, None)
        return _attn_consume_produce(s_b, s_a, kblk(j + 2), qs, vblk(j + 1), acc, *carry, None)

    s_a[...] = _dot_nt(kblk(0), qs[...])
    carry = (jnp.full((1, n), NEG_BIG, F32), jnp.zeros((1, n), F32))
    carry = lax.fori_loop(0, n_full // 2, pair, carry)
    j_tail = 2 * (n_full // 2)

    def last(buf, j, carry):
        mask = _chunk_mask(i * tq, j * tk, tq, tk)
        m, l = _attn_consume_produce(buf, None, None, qs, vblk(j), acc, *carry, mask)
        ml_s[0:1, :] = m
        ml_s[1:2, :] = l

    @pl.when(n_full % 2 == 1)
    def _():
        c2 = _attn_consume_produce(s_a, s_b, kblk(j_tail + 1), qs, vblk(j_tail), acc,
                                   *carry, None)
        last(s_b, j_tail + 1, c2)

    @pl.when(n_full % 2 == 0)
    def _():
        last(s_a, j_tail, carry)

    l = ml_s[1:2, :]

    def store(g, og):
        o_ref[0, :, g * HEAD_W:(g + 1) * HEAD_W] = og

    _attn_finish(store, acc[...], l, _lambda(lq1, lk1, lq2, lk2, lam_init), sub_ref[...],
                 tq, lam_init)


def _attn_prompt_call(q, kb, vt, lam_params, sub_gain, *, tq, tk, lam_init):
    B, L, _ = q.shape
    assert L % tq == 0 and L % tk == 0 and tq % CHUNK == 0
    assert all(((i + 1) * tq + tk - 1) // tk - (i * tq + CHUNK) // tk == 1 for i in range(L // tq))
    n = N_STACK * tq
    q_spec = pl.BlockSpec((1, tq, G_B * HEAD_W), lambda b, h, i: (b, i, h))
    small = [_const_spec((1, DH_B))] * 4 + [_const_spec((1, HEAD_W))]
    return pl.pallas_call(
        functools.partial(_attn_prompt_kernel, tq=tq, tk=tk, lam_init=lam_init),
        grid=(B, N_KV_B, L // tq),
        in_specs=[
            q_spec,
            pl.BlockSpec((1, 1, L, HEAD_W), lambda b, h, i: (b, h, 0, 0)),
            pl.BlockSpec((1, 1, HEAD_W, L), lambda b, h, i: (b, h, 0, 0)),
        ] + small,
        out_specs=q_spec,
        out_shape=jax.ShapeDtypeStruct((B, L, H_B * HEAD_W), BF16),
        scratch_shapes=[pltpu.VMEM((n, HEAD_W), BF16), pltpu.VMEM((HEAD_W, n), F32),
                        pltpu.VMEM((tk, n), F32), pltpu.VMEM((tk, n), F32),
                        pltpu.VMEM((2, n), F32)],
        compiler_params=pltpu.CompilerParams(
            dimension_semantics=("arbitrary",) * 3,
            vmem_limit_bytes=VMEM_LIMIT),
        name="attn_prompt",
    )(q, kb, vt, *lam_params, sub_gain)


def _attn_sample_kernel(q_ref, pk_ref, pv_ref, nk_ref, nv_ref, lq1, lk1, lq2, lk2, sub_ref, o_ref,
                        qst, acc, m_s, l_s, *, tq, n_past, past_len, lam_init):
    j = pl.program_id(1)

    @pl.when(j == 0)
    def _():
        q = q_ref[0]
        for h in range(N_KV_B):
            qst[h] = _stacked_q_t(q[:, h * G_B * HEAD_W:(h + 1) * G_B * HEAD_W], tq)
        acc[...] = jnp.zeros(acc.shape, F32)
        m_s[...] = jnp.full(m_s.shape, NEG_BIG, F32)
        l_s[...] = jnp.zeros(l_s.shape, F32)

    def run(k_ref, v_ref, mask):
        keys = k_ref.shape[1] // N_KV_B
        head = lambda ref, h: ref[0, pl.ds(h, keys, stride=N_KV_B), :]
        ss = [_dot(head(k_ref, h).astype(BF16), qst[h]) for h in range(N_KV_B)]
        vts = [head(v_ref, h).T.astype(BF16) for h in range(N_KV_B)]
        for h in range(N_KV_B):
            m_new, l_new = _attn_consume(ss[h], vts[h], acc.at[h], m_s[h], l_s[h], mask)
            m_s[h] = m_new
            l_s[h] = l_new

    @pl.when(j < n_past)
    def _():
        run(pk_ref, pv_ref, None)

    @pl.when(j == n_past)
    def _():
        run(nk_ref, nv_ref, _chunk_mask(past_len, past_len, tq, tq))
        lam = _lambda(lq1, lk1, lq2, lk2, lam_init)
        for h in range(N_KV_B):
            def store(g, og, h=h):
                c0 = (h * G_B + g) * HEAD_W
                o_ref[0, :, c0:c0 + HEAD_W] = og
            _attn_finish(store, acc[h], l_s[h], lam, sub_ref[...], tq, lam_init)


def _attn_sample_call(q, past_k, past_v, k_new, v_new, lam_params, sub_gain, *, tk, lam_init):
    B, tq, _ = q.shape
    past_len = past_k.shape[1]
    assert past_len % tk == 0 and past_len % CHUNK == 0 and tq <= CHUNK
    n_past = past_len // tk
    n = N_STACK * tq
    rows = lambda a: a.reshape(B, -1, HEAD_W)
    past_k, past_v, k_new, v_new = rows(past_k), rows(past_v), rows(k_new), rows(v_new)
    past_spec = pl.BlockSpec((1, tk * N_KV_B, HEAD_W), lambda b, j: (b, jnp.minimum(j, n_past - 1), 0))
    new_spec = pl.BlockSpec((1, tq * N_KV_B, HEAD_W), lambda b, j: (b, 0, 0))
    q_spec = pl.BlockSpec((1, tq, H_B * HEAD_W), lambda b, j: (b, 0, 0))
    small = [_const_spec((1, DH_B))] * 4 + [_const_spec((1, HEAD_W))]
    return pl.pallas_call(
        functools.partial(_attn_sample_kernel, tq=tq, n_past=n_past,
                          past_len=past_len, lam_init=lam_init),
        grid=(B, n_past + 1),
        in_specs=[q_spec, past_spec, past_spec, new_spec, new_spec] + small,
        out_specs=q_spec,
        out_shape=jax.ShapeDtypeStruct((B, tq, H_B * HEAD_W), BF16),
        scratch_shapes=[
            pltpu.VMEM((N_KV_B, HEAD_W, n), BF16),
            pltpu.VMEM((N_KV_B, HEAD_W, n), F32),
            pltpu.VMEM((N_KV_B, 1, n), F32),
            pltpu.VMEM((N_KV_B, 1, n), F32),
        ],
        compiler_params=pltpu.CompilerParams(
            dimension_semantics=("arbitrary",) * 2,
            vmem_limit_bytes=VMEM_LIMIT),
        name="attn_sample",
    )(q, past_k, past_v, k_new, v_new, *lam_params, sub_gain)


def _pick_tile(n, want):
    t = min(n, want)
    assert n % t == 0
    return t


def kernel(x_prompt, x_sample, state_hgrn, cache_k, cache_v, norm_mix_pre, norm_mix_post,
           norm_mlp_pre, norm_mlp_post, w_up, w_down, w_in_a, lb_logits, gnorm_a, w_out_a,
           norm_kv, w_kv, w_q_b, lam_q1, lam_k1, lam_q2, lam_k2, subln_b, w_out_b):
    row = lambda a: a.reshape(1, -1)
    w_in_bf = w_in_a[0].astype(BF16)
    w_out_a_bf = w_out_a[0].astype(BF16)
    w_up_bf = w_up.astype(BF16)
    w_down_bf = w_down.astype(BF16)
    w_kv_bf = w_kv.astype(BF16)
    w_q_bf = w_q_b[0].astype(BF16)
    w_out_b_bf = w_out_b[0].astype(BF16)
    lam_params = [row(lam_q1[0]), row(lam_k1[0]), row(lam_q2[0]), row(lam_k2[0])]
    lam_init = 0.8 - 0.6 * math.exp(-LAMBDA_INIT_SCALE * 1)

    def mlp(layer, x, a, w_o):
        return _mlp_call(x, a, w_o, row(norm_mix_post[layer]), row(norm_mlp_pre[layer]),
                         w_up_bf[layer], w_down_bf[layer], row(norm_mlp_post[layer]),
                         tile=_pick_tile(x.shape[0], 512))

    def trunk(x, s0, past_k, past_v):
        B, L, _ = x.shape
        n = B * L
        o_a, s_out = _hgrn_call(x, s0, row(norm_mix_pre[0]), w_in_bf, lb_logits, row(gnorm_a[0]),
                                tile=_pick_tile(L, 256))
        h = mlp(0, x.reshape(n, D_MODEL), o_a.reshape(n, D_A), w_out_a_bf)

        pos0 = 0 if past_k is None else past_k.shape[1]
        tile = _pick_tile(n, 512)
        reps = max(1, tile // L)
        tables = _rope_tables(pos0 + jnp.arange(L, dtype=jnp.int32))
        tables = [jnp.tile(t, (reps, 1)) for t in tables]
        outs = _qkv_call(h, row(norm_kv), row(norm_mix_pre[1]), w_kv_bf, w_q_bf, tables,
                         batch=B, tile=tile, resident=past_k is None)
        k, v, q = outs[:3]
        k = k.reshape(B, L, N_KV_B, HEAD_W)
        v = v.reshape(B, L, N_KV_B, HEAD_W)
        q3 = q.reshape(B, L, H_B * HEAD_W)
        if past_k is None:
            o_b = _attn_prompt_call(q3, outs[3], outs[4], lam_params, row(subln_b[0]),
                                    tq=512, tk=512, lam_init=lam_init)
        else:
            o_b = _attn_sample_call(q3, past_k, past_v, k, v, lam_params, row(subln_b[0]),
                                    tk=1024, lam_init=lam_init)
        y = mlp(1, h, o_b.reshape(n, D_MODEL), w_out_b_bf)
        return y.reshape(B, L, D_MODEL), s_out[None], k, v

    s0_prompt = jnp.zeros((x_prompt.shape[0], H_A, DK_A, DV_A), F32)
    y_p, s_p, k_p, v_p = trunk(x_prompt, s0_prompt, None, None)
    y_s, s_s, k_s, v_s = trunk(x_sample, state_hgrn[0], cache_k, cache_v)
    return (y_p, y_s, s_p, k_p, v_p, s_s, k_s, v_s)
```

```python
import functools
import math

import jax
import jax.numpy as jnp
from jax import lax
from jax.experimental import pallas as pl
from jax.experimental.pallas import tpu as pltpu

F32 = jnp.float32
BF16 = jnp.bfloat16

D_MODEL = 1024
CHUNK = 64
SUB = 8
N_SUB = CHUNK // SUB
H_A = 8
DK_A = 128
DV_A = 128
D_A = H_A * DK_A
H_B = 8
DH_B = 64
N_KV_B = 4
G_B = H_B // N_KV_B
HEAD_W = 2 * DH_B
ROT_DIM = DH_B // 4
ROT_HALF = ROT_DIM // 2
ROPE_THETA = 500000.0
LAMBDA_INIT_SCALE = 0.3
D_FF = 4 * D_MODEL
FF_BLOCK = 1024
EPS = 1e-6
NEG_BIG = -1e30
LANE = 128
LOG2_E = math.log2(math.e)

VMEM_LIMIT = 52 * 1024 * 1024


def _rms(x):
    return x * lax.rsqrt(jnp.mean(x * x, axis=-1, keepdims=True) + EPS)


def _sigmoid(x):
    return 1.0 / (1.0 + jnp.exp(-x))


def _dot(a, b):
    return jnp.dot(a, b, preferred_element_type=F32)


def _dot_nt(a, b):
    return lax.dot_general(a, b, (((1,), (1,)), ((), ())), preferred_element_type=F32)


def _const_spec(shape):
    nd = len(shape)
    return pl.BlockSpec(shape, lambda *_: (0,) * nd)


def _cumsum_rows(tri, g):
    hi = g.astype(BF16)
    r1 = g - hi.astype(F32)
    mid = r1.astype(BF16)
    lo = (r1 - mid.astype(F32)).astype(BF16)
    return _dot(tri, hi) + _dot(tri, mid) + _dot(tri, lo)


def _hgrn_consts():
    rows = lax.broadcasted_iota(jnp.int32, (CHUNK, CHUNK), 0)
    cols = lax.broadcasted_iota(jnp.int32, (CHUNK, CHUNK), 1)
    tri = (rows >= cols).astype(BF16)
    n_off = SUB * (N_SUB * (N_SUB - 1) // 2)
    r_blk = lax.broadcasted_iota(jnp.int32, (CHUNK, n_off), 0) // SUB
    c_idx = lax.broadcasted_iota(jnp.int32, (CHUNK, n_off), 1)
    c_grp = jnp.zeros((CHUNK, n_off), jnp.int32)
    start = 0
    for i in range(1, N_SUB):
        c_grp = jnp.where(c_idx >= start, i, c_grp)
        start += SUB * i
    causal = (lax.broadcasted_iota(jnp.int32, (CHUNK, LANE), 0)
              >= lax.broadcasted_iota(jnp.int32, (CHUNK, LANE), 1))
    return tri, r_blk == c_grp, causal


def _hgrn_chunk(q, k, g2, v, st, tri, own_group, causal, c_ref):
    heads = [slice(h * DK_A, (h + 1) * DK_A) for h in range(H_A)]
    b = _cumsum_rows(tri, g2)
    b_last = b[CHUNK - 1:CHUNK, :]
    c = b - jnp.log2(k)
    for h, hs in enumerate(heads):
        c_ref[h] = c[:, hs]

    refs = [b[SUB * i - 1:SUB * i, :] for i in range(1, N_SUB)]
    ref_rows = jnp.concatenate(
        [jnp.zeros((SUB, D_A), F32)] + [jnp.broadcast_to(r, (SUB, D_A)) for r in refs], axis=0)
    qe = (q * jnp.exp2(b)).astype(BF16)
    qd = (q * jnp.exp2(b - ref_rows)).astype(BF16)
    kd = jnp.concatenate(
        [jnp.exp2(refs[i - 1] - c[:SUB * i]) for i in range(1, N_SUB)], axis=0).astype(BF16)
    v_bf = v.astype(BF16)
    v_all = jnp.concatenate([v_bf[:SUB * i] for i in range(1, N_SUB)], axis=0)
    kdl = jnp.exp2(b_last - c).astype(BF16)
    keep = jnp.exp2(b_last)

    o = [_dot_nt(qe[:, hs], st[h].astype(BF16)) for h, hs in enumerate(heads)]
    sc = [_dot_nt(qd[:, hs], kd[:, hs]) for hs in heads]
    sc = [jnp.where(own_group, x, 0.0).astype(BF16) for x in sc]
    o = [o[h] + _dot(sc[h], v_all[:, hs]) for h, hs in enumerate(heads)]
    st_new = [st[h] * keep[:, hs] + _dot(v[:, hs].T.astype(BF16), kdl[:, hs])
              for h, hs in enumerate(heads)]

    lane = lax.broadcasted_iota(jnp.int32, (SUB, LANE), 1)
    a_rows = [[] for _ in heads]
    for i in range(N_SUB):
        sl = slice(SUB * i, SUB * (i + 1))
        q_r, b_r = q[sl], b[sl]
        a_i = [jnp.zeros((SUB, LANE), F32) for _ in heads]
        for s in range(SUB):
            for h, hs in enumerate(heads):
                c_s = c_ref[h, pl.ds(SUB * i + s, SUB, stride=0), :]
                w = q_r[:, hs] * jnp.exp2(b_r[:, hs] - c_s)
                col = jnp.sum(w, axis=-1, keepdims=True)
                a_i[h] = jnp.where(lane == SUB * i + s, col, a_i[h])
        for h in range(H_A):
            a_rows[h].append(a_i[h])
    v_pad = jnp.concatenate([v_bf, jnp.zeros((LANE - CHUNK, D_A), BF16)], axis=0)
    for h, hs in enumerate(heads):
        a = jnp.where(causal, jnp.concatenate(a_rows[h], axis=0), 0.0)
        o[h] = o[h] + _dot(a.astype(BF16), v_pad[:, hs])
    return o, st_new


def _forget_floor(lbl_ref):
    lbl = lbl_ref[...]
    e = jnp.exp(lbl - jnp.max(lbl, axis=0, keepdims=True))
    return e[0:1, :] / jnp.sum(e, axis=0, keepdims=True)


N_PROJ = 4


def _hgrn_project(stage, u, win_ref, lb, dst):
    q_s, k_s, g_s, v_s, gate_s = dst
    y = _dot(u, win_ref[:, stage * D_A:(stage + 1) * D_A])
    if stage == 0:
        q_s[...] = y * _sigmoid(y) * (DK_A ** -0.5)
    elif stage == 1:
        forget = lb + (1.0 - lb) * _sigmoid(y)
        k_s[...] = jnp.maximum(1.0 - forget, 0.0)
        g_s[...] = jnp.log2(forget)
    elif stage == 2:
        v_s[...] = y
    else:
        gate_s[...] = y * _sigmoid(y)


def _hgrn_run_chunk(src, rs, st_s, c_s, consts, gg, o_ref, out_rs):
    q_s, k_s, g_s, v_s, gate_s = src
    o, st_new = _hgrn_chunk(q_s[rs, :], k_s[rs, :], g_s[rs, :], v_s[rs, :],
                            [st_s[h] for h in range(H_A)], *consts, c_s)
    for h in range(H_A):
        hs = slice(h * DV_A, (h + 1) * DV_A)
        st_s[h] = st_new[h]
        o_ref[0, out_rs, hs] = (_rms(o[h]) * gg * gate_s[rs, hs]).astype(BF16)


def _hgrn_kernel(x_ref, s0_ref, gpre_ref, win_ref, lbl_ref, gg_ref,
                 o_ref, sout_ref,
                 q_s, k_s, g_s, v_s, gate_s, st_s, c_s, *, tile):
    t = pl.program_id(1)

    @pl.when(t == 0)
    def _():
        for h in range(H_A):
            st_s[h] = s0_ref[0, h].T

    lb = _forget_floor(lbl_ref)
    rows = (q_s, k_s, g_s, v_s, gate_s)
    u = (_rms(x_ref[0]) * gpre_ref[...]).astype(BF16)
    for stage in range(N_PROJ):
        _hgrn_project(stage, u, win_ref, lb, rows)

    gg = gg_ref[...]
    consts = _hgrn_consts()

    def chunk_body(c, carry):
        rs = pl.ds(pl.multiple_of(c * CHUNK, CHUNK), CHUNK)
        _hgrn_run_chunk(rows, rs, st_s, c_s, consts, gg, o_ref, rs)
        return carry

    lax.fori_loop(0, tile // CHUNK, chunk_body, 0)

    @pl.when(t == pl.num_programs(1) - 1)
    def _():
        for h in range(H_A):
            sout_ref[0, h] = st_s[h].T


def _hgrn_pipe_kernel(x_ref, xn_ref, s0_ref, gpre_ref, win_ref, lbl_ref, gg_ref,
                      o_ref, sout_ref, *scratch, tile):
    set_a, set_b = scratch[0:5], scratch[5:10]
    st_s, c_s = scratch[10:12]
    t = pl.program_id(1)
    lb = _forget_floor(lbl_ref)
    gpre = gpre_ref[...]
    gg = gg_ref[...]
    consts = _hgrn_consts()
    norm_in = lambda x: (_rms(x) * gpre).astype(BF16)

    @pl.when(t == 0)
    def _():
        for h in range(H_A):
            st_s[h] = s0_ref[0, h].T
        u0 = norm_in(x_ref[0, 0:tile, :])
        for stage in range(N_PROJ):
            _hgrn_project(stage, u0, win_ref, lb, set_a)

    def phase(src, x_next, dst, out_row0):
        u = norm_in(x_next)
        for c in range(tile // CHUNK):
            rs = slice(c * CHUNK, (c + 1) * CHUNK)
            out_rs = slice(out_row0 + c * CHUNK, out_row0 + (c + 1) * CHUNK)
            _hgrn_run_chunk(src, rs, st_s, c_s, consts, gg, o_ref, out_rs)
            _hgrn_project(c, u, win_ref, lb, dst)

    phase(set_a, x_ref[0, tile:2 * tile, :], set_b, 0)
    phase(set_b, xn_ref[0], set_a, tile)

    @pl.when(t == pl.num_programs(1) - 1)
    def _():
        for h in range(H_A):
            sout_ref[0, h] = st_s[h].T


def _hgrn_call(x, s0, gpre, w_in, lb_logits, ggain, *, tile):
    B, L, _ = x.shape
    pipelined = L % (2 * tile) == 0 and tile // CHUNK == N_PROJ
    row_scratch = pltpu.VMEM((tile, D_A), F32)
    state_spec = pl.BlockSpec((1, H_A, DK_A, DV_A), lambda b, t: (b, 0, 0, 0))
    consts = [_const_spec((1, D_MODEL)), _const_spec((D_MODEL, 4 * D_A)),
              _const_spec(lb_logits.shape), _const_spec((1, DV_A))]
    if pipelined:
        step = 2 * tile
        last_tile = L // tile - 1
        kern = functools.partial(_hgrn_pipe_kernel, tile=tile)
        x_specs = [pl.BlockSpec((1, step, D_MODEL), lambda b, t: (b, t, 0)),
                   pl.BlockSpec((1, tile, D_MODEL),
                                lambda b, t: (b, jnp.minimum(2 * t + 2, last_tile), 0))]
        x_args = (x, x)
        n_sets = 2
    else:
        step = tile
        kern = functools.partial(_hgrn_kernel, tile=tile)
        x_specs = [pl.BlockSpec((1, tile, D_MODEL), lambda b, t: (b, t, 0))]
        x_args = (x,)
        n_sets = 1
    return pl.pallas_call(
        kern,
        grid=(B, L // step),
        in_specs=x_specs + [state_spec] + consts,
        out_specs=[pl.BlockSpec((1, step, D_A), lambda b, t: (b, t, 0)), state_spec],
        out_shape=[
            jax.ShapeDtypeStruct((B, L, D_A), BF16),
            jax.ShapeDtypeStruct((B, H_A, DK_A, DV_A), F32),
        ],
        scratch_shapes=[row_scratch] * (5 * n_sets) + [pltpu.VMEM((H_A, DV_A, DK_A), F32),
                                                       pltpu.VMEM((H_A, CHUNK, DK_A), F32)],
        compiler_params=pltpu.CompilerParams(
            dimension_semantics=("arbitrary", "arbitrary"),
            vmem_limit_bytes=VMEM_LIMIT),
        name="hgrn",
    )(*x_args, s0, gpre, w_in, lb_logits, ggain)


def _mlp_kernel(x_ref, a_ref, wo_ref, gpost_ref, gpre_ref, wup_ref, wdn_ref, gmpost_ref, y_ref):
    h = x_ref[...] + _rms(_dot(a_ref[...], wo_ref[...])) * gpost_ref[...]
    hn = (_rms(h) * gpre_ref[...]).astype(BF16)
    acc = jnp.zeros(h.shape, F32)
    for c in range(D_FF // FF_BLOCK):
        sl = slice(c * FF_BLOCK, (c + 1) * FF_BLOCK)
        u = jnp.maximum(_dot(hn, wup_ref[:, sl]), 0.0)
        acc = acc + _dot((u * u).astype(BF16), wdn_ref[sl, :])
    y_ref[...] = h + _rms(acc) * gmpost_ref[...]


def _mlp_call(x, a, w_o, gpost, gpre, w_up, w_down, gmpost, *, tile):
    n = x.shape[0]
    row = lambda i: (i, 0)
    return pl.pallas_call(
        _mlp_kernel,
        grid=(n // tile,),
        in_specs=[
            pl.BlockSpec((tile, D_MODEL), row),
            pl.BlockSpec((tile, D_MODEL), row),
            _const_spec((D_MODEL, D_MODEL)),
            _const_spec((1, D_MODEL)),
            _const_spec((1, D_MODEL)),
            _const_spec((D_MODEL, D_FF)),
            _const_spec((D_FF, D_MODEL)),
            _const_spec((1, D_MODEL)),
        ],
        out_specs=pl.BlockSpec((tile, D_MODEL), row),
        out_shape=jax.ShapeDtypeStruct((n, D_MODEL), F32),
        compiler_params=pltpu.CompilerParams(
            dimension_semantics=("arbitrary",),
            vmem_limit_bytes=VMEM_LIMIT),
        name="mlp",
    )(x, a, w_o, gpost, gpre, w_up, w_down, gmpost)


def _rope_tables(pos):
    lane = jnp.arange(LANE, dtype=jnp.int32) % DH_B
    pair = (2 * (lane % ROT_HALF)).astype(F32)
    inv_freq = jnp.where(lane < ROT_DIM, ROPE_THETA ** (-pair / ROT_DIM), 0.0)
    ang = pos.astype(F32)[:, None] * inv_freq[None, :]
    cos, sin = jnp.cos(ang), jnp.sin(ang)
    up = jnp.where(lane < ROT_HALF, -sin, 0.0)
    dn = jnp.where((lane >= ROT_HALF) & (lane < ROT_DIM), sin, 0.0)
    return cos, up, dn


def _qkv_kernel(h_ref, gkv_ref, gq_ref, wkv_ref, wq_ref, cos_ref, up_ref, dn_ref,
                k_ref, v_ref, q_ref, *resident):
    tile = h_ref.shape[0]
    xh = _rms(h_ref[...])
    kv = _dot((xh * gkv_ref[...]).astype(BF16), wkv_ref[...])
    q = _dot((xh * gq_ref[...]).astype(BF16), wq_ref[...])
    cos, up, dn = cos_ref[...], up_ref[...], dn_ref[...]

    def rope(x):
        return (x * cos + pltpu.roll(x, LANE - ROT_HALF, axis=1) * up
                + pltpu.roll(x, ROT_HALF, axis=1) * dn)

    n_k = N_KV_B * HEAD_W
    for j in range(N_KV_B):
        kj = rope(kv[:, j * HEAD_W:(j + 1) * HEAD_W])
        vj = kv[:, n_k + j * HEAD_W:n_k + (j + 1) * HEAD_W]
        k_ref[pl.ds(j, tile, stride=N_KV_B), :] = kj
        v_ref[pl.ds(j, tile, stride=N_KV_B), :] = vj
        if resident:
            kb_ref, vt_ref = resident
            kb_ref[0, j] = kj.astype(BF16)
            vt_ref[0, j] = vj.T.astype(BF16)
    for j in range(H_B):
        sl = slice(j * HEAD_W, (j + 1) * HEAD_W)
        q_ref[:, sl] = (rope(q[:, sl]) * (DH_B ** -0.5 * LOG2_E)).astype(BF16)


def _qkv_call(h, gkv, gq, w_kv, w_q, tables, *, batch, tile, resident):
    n = h.shape[0]
    seq = n // batch
    per_seq = max(1, seq // tile)
    n_tab = tables[0].shape[0] // tile
    row = lambda i: (i, 0)
    tab = pl.BlockSpec((tile, LANE), lambda i: (i % n_tab, 0))
    n_k = N_KV_B * HEAD_W
    assert seq % tile == 0 or not resident
    kv_spec = pl.BlockSpec((tile * N_KV_B, HEAD_W), row)
    out_specs = [kv_spec, kv_spec, pl.BlockSpec((tile, H_B * HEAD_W), row)]
    out_shape = [
        jax.ShapeDtypeStruct((n * N_KV_B, HEAD_W), F32),
        jax.ShapeDtypeStruct((n * N_KV_B, HEAD_W), F32),
        jax.ShapeDtypeStruct((n, H_B * HEAD_W), BF16),
    ]
    if resident:
        out_specs += [
            pl.BlockSpec((1, N_KV_B, tile, HEAD_W), lambda i: (i // per_seq, 0, i % per_seq, 0)),
            pl.BlockSpec((1, N_KV_B, HEAD_W, tile), lambda i: (i // per_seq, 0, 0, i % per_seq)),
        ]
        out_shape += [
            jax.ShapeDtypeStruct((batch, N_KV_B, seq, HEAD_W), BF16),
            jax.ShapeDtypeStruct((batch, N_KV_B, HEAD_W, seq), BF16),
        ]
    return pl.pallas_call(
        _qkv_kernel,
        grid=(n // tile,),
        in_specs=[
            pl.BlockSpec((tile, D_MODEL), row),
            _const_spec((1, D_MODEL)),
            _const_spec((1, D_MODEL)),
            _const_spec((D_MODEL, 2 * n_k)),
            _const_spec((D_MODEL, H_B * HEAD_W)),
            tab, tab, tab,
        ],
        out_specs=out_specs,
        out_shape=out_shape,
        compiler_params=pltpu.CompilerParams(
            dimension_semantics=("arbitrary",),
            vmem_limit_bytes=VMEM_LIMIT),
        name="qkv",
    )(h, gkv, gq, w_kv, w_q, *tables)


N_STACK = 2 * G_B
COL_GROUP = 256
PV_LAG = 1


def _stacked_q(q, tq):
    lane = lax.broadcasted_iota(jnp.int32, (tq, HEAD_W), 1)
    zero = jnp.zeros((tq, HEAD_W), BF16)
    parts = []
    for g in range(G_B):
        qg = q[:, g * HEAD_W:(g + 1) * HEAD_W]
        parts.append(jnp.where(lane < DH_B, qg, zero))
        parts.append(jnp.where(lane >= DH_B, qg, zero))
    return jnp.concatenate(parts, axis=0)


def _stacked_q_t(q, tq):
    lane = lax.broadcasted_iota(jnp.int32, (tq, HEAD_W), 1)
    parts = []
    for g in range(G_B):
        qg = q[:, g * HEAD_W:(g + 1) * HEAD_W].astype(F32)
        parts.append(jnp.where(lane < DH_B, qg, 0.0))
        parts.append(jnp.where(lane >= DH_B, qg, 0.0))
    return jnp.concatenate(parts, axis=0).T.astype(BF16)


def _attn_consume(s, vt_blk, acc, m_prev, l_prev, mask):
    if mask is not None:
        s = jnp.where(mask, s, NEG_BIG)
    m_new = jnp.maximum(m_prev, jnp.max(s, axis=0, keepdims=True))
    alpha = jnp.exp2(m_prev - m_new)
    p = jnp.exp2(s - m_new)
    l_new = alpha * l_prev + jnp.sum(p, axis=0, keepdims=True)
    p = p.astype(BF16)
    n = s.shape[1]
    for c in range(n // COL_GROUP):
        cs = slice(c * COL_GROUP, (c + 1) * COL_GROUP)
        acc[:, cs] = acc[:, cs] * alpha[:, cs] + _dot(vt_blk, p[:, cs])
    return m_new, l_new


def _attn_consume_produce(src, dst, k_next, qs, vt_blk, acc, m_prev, l_prev, mask):
    n = src.shape[1]
    n_grp = n // COL_GROUP
    cs = [slice(c * COL_GROUP, (c + 1) * COL_GROUP) for c in range(n_grp)]
    m_out, l_out, p, alpha = [], [], [], []
    for c in range(n_grp + PV_LAG):
        if c < n_grp:
            s = src[:, cs[c]]
            if mask is not None:
                s = jnp.where(mask[:, cs[c]], s, NEG_BIG)
            m_new = jnp.maximum(m_prev[:, cs[c]], jnp.max(s, axis=0, keepdims=True))
            alpha.append(jnp.exp2(m_prev[:, cs[c]] - m_new))
            e = jnp.exp2(s - m_new)
            l_out.append(alpha[c] * l_prev[:, cs[c]] + jnp.sum(e, axis=0, keepdims=True))
            m_out.append(m_new)
            p.append(e.astype(BF16))
            if k_next is not None:
                dst[:, cs[c]] = _dot_nt(k_next, qs[cs[c], :])
        d = c - PV_LAG
        if d >= 0:
            acc[:, cs[d]] = acc[:, cs[d]] * alpha[d] + _dot(vt_blk, p[d])
    return jnp.concatenate(m_out, axis=1), jnp.concatenate(l_out, axis=1)


def _chunk_mask(q_pos0, k_pos0, tq, tk):
    c = lax.broadcasted_iota(jnp.int32, (1, N_STACK * tq), 1)
    qc = (q_pos0 + c % tq) // CHUNK
    kc = (k_pos0 + lax.broadcasted_iota(jnp.int32, (tk, 1), 0)) // CHUNK
    return kc <= qc


def _lambda(lq1, lk1, lq2, lk2, lam_init):
    return (jnp.exp(jnp.sum(lq1[...] * lk1[...], axis=1, keepdims=True))
            - jnp.exp(jnp.sum(lq2[...] * lk2[...], axis=1, keepdims=True)) + lam_init)


def _attn_finish(store, acc, l, lam, sub, tq, lam_init):
    o = (acc * (1.0 / l)).T
    for g in range(G_B):
        og = o[(2 * g) * tq:(2 * g + 1) * tq] - lam * o[(2 * g + 1) * tq:(2 * g + 2) * tq]
        store(g, (_rms(og) * sub * (1.0 - lam_init)).astype(BF16))


def _attn_prompt_kernel(q_ref, kb_ref, vt_ref, lq1, lk1, lq2, lk2, sub_ref, o_ref,
                        qs, acc, s_a, s_b, ml_s, *, tq, tk, lam_init):
    i = pl.program_id(2)
    n = N_STACK * tq
    qs[...] = _stacked_q(q_ref[0], tq)
    acc[...] = jnp.zeros(acc.shape, F32)

    n_full = (i * tq + CHUNK) // tk

    def kblk(j):
        return kb_ref[0, 0, pl.ds(pl.multiple_of(j * tk, tk), tk), :]

    def vblk(j):
        return vt_ref[0, 0, :, pl.ds(pl.multiple_of(j * tk, tk), tk)]

    def pair(jj, carry):
        j = 2 * jj
        carry = _attn_consume_produce(s_a, s_b, kblk(j + 1), qs, vblk(j), acc, *carry, None)
        return _attn_consume_produce(s_b, s_a, kblk(j + 2), qs, vblk(j + 1), acc, *carry, None)

    s_a[...] = _dot_nt(kblk(0), qs[...])
    carry = (jnp.full((1, n), NEG_BIG, F32), jnp.zeros((1, n), F32))
    carry = lax.fori_loop(0, n_full // 2, pair, carry)
    j_tail = 2 * (n_full // 2)

    def last(buf, j, carry):
        mask = _chunk_mask(i * tq, j * tk, tq, tk)
        m, l = _attn_consume_produce(buf, None, None, qs, vblk(j), acc, *carry, mask)
        ml_s[0:1, :] = m
        ml_s[1:2, :] = l

    @pl.when(n_full % 2 == 1)
    def _():
        c2 = _attn_consume_produce(s_a, s_b, kblk(j_tail + 1), qs, vblk(j_tail), acc,
                                   *carry, None)
        last(s_b, j_tail + 1, c2)

    @pl.when(n_full % 2 == 0)
    def _():
        last(s_a, j_tail, carry)

    l = ml_s[1:2, :]

    def store(g, og):
        o_ref[0, :, g * HEAD_W:(g + 1) * HEAD_W] = og

    _attn_finish(store, acc[...], l, _lambda(lq1, lk1, lq2, lk2, lam_init), sub_ref[...],
                 tq, lam_init)


def _attn_prompt_call(q, kb, vt, lam_params, sub_gain, *, tq, tk, lam_init):
    B, L, _ = q.shape
    assert L % tq == 0 and L % tk == 0 and tq % CHUNK == 0
    assert all(((i + 1) * tq + tk - 1) // tk - (i * tq + CHUNK) // tk == 1 for i in range(L // tq))
    n = N_STACK * tq
    q_spec = pl.BlockSpec((1, tq, G_B * HEAD_W), lambda b, h, i: (b, i, h))
    small = [_const_spec((1, DH_B))] * 4 + [_const_spec((1, HEAD_W))]
    return pl.pallas_call(
        functools.partial(_attn_prompt_kernel, tq=tq, tk=tk, lam_init=lam_init),
        grid=(B, N_KV_B, L // tq),
        in_specs=[
            q_spec,
            pl.BlockSpec((1, 1, L, HEAD_W), lambda b, h, i: (b, h, 0, 0)),
            pl.BlockSpec((1, 1, HEAD_W, L), lambda b, h, i: (b, h, 0, 0)),
        ] + small,
        out_specs=q_spec,
        out_shape=jax.ShapeDtypeStruct((B, L, H_B * HEAD_W), BF16),
        scratch_shapes=[pltpu.VMEM((n, HEAD_W), BF16), pltpu.VMEM((HEAD_W, n), F32),
                        pltpu.VMEM((tk, n), F32), pltpu.VMEM((tk, n), F32),
                        pltpu.VMEM((2, n), F32)],
        compiler_params=pltpu.CompilerParams(
            dimension_semantics=("arbitrary",) * 3,
            vmem_limit_bytes=VMEM_LIMIT),
        name="attn_prompt",
    )(q, kb, vt, *lam_params, sub_gain)


def _attn_sample_kernel(q_ref, pk_ref, pv_ref, nk_ref, nv_ref, lq1, lk1, lq2, lk2, sub_ref, o_ref,
                        qst, acc, m_s, l_s, *, tq, n_past, past_len, lam_init):
    j = pl.program_id(1)

    @pl.when(j == 0)
    def _():
        q = q_ref[0]
        for h in range(N_KV_B):
            qst[h] = _stacked_q_t(q[:, h * G_B * HEAD_W:(h + 1) * G_B * HEAD_W], tq)
        acc[...] = jnp.zeros(acc.shape, F32)
        m_s[...] = jnp.full(m_s.shape, NEG_BIG, F32)
        l_s[...] = jnp.zeros(l_s.shape, F32)

    def run(k_ref, v_ref, mask):
        keys = k_ref.shape[1] // N_KV_B
        head = lambda ref, h: ref[0, pl.ds(h, keys, stride=N_KV_B), :]
        ss = [_dot(head(k_ref, h).astype(BF16), qst[h]) for h in range(N_KV_B)]
        vts = [head(v_ref, h).T.astype(BF16) for h in range(N_KV_B)]
        for h in range(N_KV_B):
            m_new, l_new = _attn_consume(ss[h], vts[h], acc.at[h], m_s[h], l_s[h], mask)
            m_s[h] = m_new
            l_s[h] = l_new

    @pl.when(j < n_past)
    def _():
        run(pk_ref, pv_ref, None)

    @pl.when(j == n_past)
    def _():
        run(nk_ref, nv_ref, _chunk_mask(past_len, past_len, tq, tq))
        lam = _lambda(lq1, lk1, lq2, lk2, lam_init)
        for h in range(N_KV_B):
            def store(g, og, h=h):
                c0 = (h * G_B + g) * HEAD_W
                o_ref[0, :, c0:c0 + HEAD_W] = og
            _attn_finish(store, acc[h], l_s[h], lam, sub_ref[...], tq, lam_init)


def _attn_sample_call(q, past_k, past_v, k_new, v_new, lam_params, sub_gain, *, tk, lam_init):
    B, tq, _ = q.shape
    past_len = past_k.shape[1]
    assert past_len % tk == 0 and past_len % CHUNK == 0 and tq <= CHUNK
    n_past = past_len // tk
    n = N_STACK * tq
    rows = lambda a: a.reshape(B, -1, HEAD_W)
    past_k, past_v, k_new, v_new = rows(past_k), rows(past_v), rows(k_new), rows(v_new)
    past_spec = pl.BlockSpec((1, tk * N_KV_B, HEAD_W), lambda b, j: (b, jnp.minimum(j, n_past - 1), 0))
    new_spec = pl.BlockSpec((1, tq * N_KV_B, HEAD_W), lambda b, j: (b, 0, 0))
    q_spec = pl.BlockSpec((1, tq, H_B * HEAD_W), lambda b, j: (b, 0, 0))
    small = [_const_spec((1, DH_B))] * 4 + [_const_spec((1, HEAD_W))]
    return pl.pallas_call(
        functools.partial(_attn_sample_kernel, tq=tq, n_past=n_past,
                          past_len=past_len, lam_init=lam_init),
        grid=(B, n_past + 1),
        in_specs=[q_spec, past_spec, past_spec, new_spec, new_spec] + small,
        out_specs=q_spec,
        out_shape=jax.ShapeDtypeStruct((B, tq, H_B * HEAD_W), BF16),
        scratch_shapes=[
            pltpu.VMEM((N_KV_B, HEAD_W, n), BF16),
            pltpu.VMEM((N_KV_B, HEAD_W, n), F32),
            pltpu.VMEM((N_KV_B, 1, n), F32),
            pltpu.VMEM((N_KV_B, 1, n), F32),
        ],
        compiler_params=pltpu.CompilerParams(
            dimension_semantics=("arbitrary",) * 2,
            vmem_limit_bytes=VMEM_LIMIT),
        name="attn_sample",
    )(q, past_k, past_v, k_new, v_new, *lam_params, sub_gain)


def _pick_tile(n, want):
    t = min(n, want)
    assert n % t == 0
    return t


def kernel(x_prompt, x_sample, state_hgrn, cache_k, cache_v, norm_mix_pre, norm_mix_post,
           norm_mlp_pre, norm_mlp_post, w_up, w_down, w_in_a, lb_logits, gnorm_a, w_out_a,
           norm_kv, w_kv, w_q_b, lam_q1, lam_k1, lam_q2, lam_k2, subln_b, w_out_b):
    row = lambda a: a.reshape(1, -1)
    w_in_bf = w_in_a[0].astype(BF16)
    w_out_a_bf = w_out_a[0].astype(BF16)
    w_up_bf = w_up.astype(BF16)
    w_down_bf = w_down.astype(BF16)
    w_kv_bf = w_kv.astype(BF16)
    w_q_bf = w_q_b[0].astype(BF16)
    w_out_b_bf = w_out_b[0].astype(BF16)
    lam_params = [row(lam_q1[0]), row(lam_k1[0]), row(lam_q2[0]), row(lam_k2[0])]
    lam_init = 0.8 - 0.6 * math.exp(-LAMBDA_INIT_SCALE * 1)

    def mlp(layer, x, a, w_o):
        return _mlp_call(x, a, w_o, row(norm_mix_post[layer]), row(norm_mlp_pre[layer]),
                         w_up_bf[layer], w_down_bf[layer], row(norm_mlp_post[layer]),
                         tile=_pick_tile(x.shape[0], 512))

    def trunk(x, s0, past_k, past_v):
        B, L, _ = x.shape
        n = B * L
        o_a, s_out = _hgrn_call(x, s0, row(norm_mix_pre[0]), w_in_bf, lb_logits, row(gnorm_a[0]),
                                tile=_pick_tile(L, 256))
        h = mlp(0, x.reshape(n, D_MODEL), o_a.reshape(n, D_A), w_out_a_bf)

        pos0 = 0 if past_k is None else past_k.shape[1]
        tile = _pick_tile(n, 512)
        reps = max(1, tile // L)
        tables = _rope_tables(pos0 + jnp.arange(L, dtype=jnp.int32))
        tables = [jnp.tile(t, (reps, 1)) for t in tables]
        outs = _qkv_call(h, row(norm_kv), row(norm_mix_pre[1]), w_kv_bf, w_q_bf, tables,
                         batch=B, tile=tile, resident=past_k is None)
        k, v, q = outs[:3]
        k = k.reshape(B, L, N_KV_B, HEAD_W)
        v = v.reshape(B, L, N_KV_B, HEAD_W)
        q3 = q.reshape(B, L, H_B * HEAD_W)
        if past_k is None:
            o_b = _attn_prompt_call(q3, outs[3], outs[4], lam_params, row(subln_b[0]),
                                    tq=512, tk=512, lam_init=lam_init)
        else:
            o_b = _attn_sample_call(q3, past_k, past_v, k, v, lam_params, row(subln_b[0]),
                                    tk=1024, lam_init=lam_init)
        y = mlp(1, h, o_b.reshape(n, D_MODEL), w_out_b_bf)
        return y.reshape(B, L, D_MODEL), s_out[None], k, v

    s0_prompt = jnp.zeros((x_prompt.shape[0], H_A, DK_A, DV_A), F32)
    y_p, s_p, k_p, v_p = trunk(x_prompt, s0_prompt, None, None)
    y_s, s_s, k_s, v_s = trunk(x_sample, state_hgrn[0], cache_k, cache_v)
    return (y_p, y_s, s_p, k_p, v_p, s_s, k_s, v_s)
```

```python
import functools
import math

import jax
import jax.numpy as jnp
from jax import lax
from jax.experimental import pallas as pl
from jax.experimental.pallas import tpu as pltpu

F32 = jnp.float32
BF16 = jnp.bfloat16

D_MODEL = 1024
CHUNK = 64
SUB = 8
N_SUB = CHUNK // SUB
H_A = 8
DK_A = 128
DV_A = 128
D_A = H_A * DK_A
H_B = 8
DH_B = 64
N_KV_B = 4
G_B = H_B // N_KV_B
HEAD_W = 2 * DH_B
ROT_DIM = DH_B // 4
ROT_HALF = ROT_DIM // 2
ROPE_THETA = 500000.0
LAMBDA_INIT_SCALE = 0.3
D_FF = 4 * D_MODEL
FF_BLOCK = 1024
EPS = 1e-6
NEG_BIG = -1e30
LANE = 128
LOG2_E = math.log2(math.e)

VMEM_LIMIT = 52 * 1024 * 1024


def _rms(x):
    return x * lax.rsqrt(jnp.mean(x * x, axis=-1, keepdims=True) + EPS)


def _sigmoid(x):
    return 1.0 / (1.0 + jnp.exp(-x))


def _dot(a, b):
    return jnp.dot(a, b, preferred_element_type=F32)


def _dot_nt(a, b):
    return lax.dot_general(a, b, (((1,), (1,)), ((), ())), preferred_element_type=F32)


def _const_spec(shape):
    nd = len(shape)
    return pl.BlockSpec(shape, lambda *_: (0,) * nd)


def _cumsum_rows(tri, g):
    hi = g.astype(BF16)
    r1 = g - hi.astype(F32)
    mid = r1.astype(BF16)
    lo = (r1 - mid.astype(F32)).astype(BF16)
    return _dot(tri, hi) + _dot(tri, mid) + _dot(tri, lo)


def _hgrn_consts():
    rows = lax.broadcasted_iota(jnp.int32, (CHUNK, CHUNK), 0)
    cols = lax.broadcasted_iota(jnp.int32, (CHUNK, CHUNK), 1)
    tri = (rows >= cols).astype(BF16)
    n_off = SUB * (N_SUB * (N_SUB - 1) // 2)
    r_blk = lax.broadcasted_iota(jnp.int32, (CHUNK, n_off), 0) // SUB
    c_idx = lax.broadcasted_iota(jnp.int32, (CHUNK, n_off), 1)
    c_grp = jnp.zeros((CHUNK, n_off), jnp.int32)
    start = 0
    for i in range(1, N_SUB):
        c_grp = jnp.where(c_idx >= start, i, c_grp)
        start += SUB * i
    causal = (lax.broadcasted_iota(jnp.int32, (CHUNK, LANE), 0)
              >= lax.broadcasted_iota(jnp.int32, (CHUNK, LANE), 1))
    return tri, r_blk == c_grp, causal


def _hgrn_chunk(q, k, g2, v, st, tri, own_group, causal, c_ref):
    heads = [slice(h * DK_A, (h + 1) * DK_A) for h in range(H_A)]
    b = _cumsum_rows(tri, g2)
    b_last = b[CHUNK - 1:CHUNK, :]
    c = b - jnp.log2(k)
    for h, hs in enumerate(heads):
        c_ref[h] = c[:, hs]

    refs = [b[SUB * i - 1:SUB * i, :] for i in range(1, N_SUB)]
    ref_rows = jnp.concatenate(
        [jnp.zeros((SUB, D_A), F32)] + [jnp.broadcast_to(r, (SUB, D_A)) for r in refs], axis=0)
    qe = (q * jnp.exp2(b)).astype(BF16)
    qd = (q * jnp.exp2(b - ref_rows)).astype(BF16)
    kd = jnp.concatenate(
        [jnp.exp2(refs[i - 1] - c[:SUB * i]) for i in range(1, N_SUB)], axis=0).astype(BF16)
    v_bf = v.astype(BF16)
    v_all = jnp.concatenate([v_bf[:SUB * i] for i in range(1, N_SUB)], axis=0)
    kdl = jnp.exp2(b_last - c).astype(BF16)
    keep = jnp.exp2(b_last)

    o = [_dot_nt(qe[:, hs], st[h].astype(BF16)) for h, hs in enumerate(heads)]
    sc = [_dot_nt(qd[:, hs], kd[:, hs]) for hs in heads]
    sc = [jnp.where(own_group, x, 0.0).astype(BF16) for x in sc]
    o = [o[h] + _dot(sc[h], v_all[:, hs]) for h, hs in enumerate(heads)]
    st_new = [st[h] * keep[:, hs] + _dot(v[:, hs].T.astype(BF16), kdl[:, hs])
              for h, hs in enumerate(heads)]

    lane = lax.broadcasted_iota(jnp.int32, (SUB, LANE), 1)
    a_rows = [[] for _ in heads]
    for i in range(N_SUB):
        sl = slice(SUB * i, SUB * (i + 1))
        q_r, b_r = q[sl], b[sl]
        a_i = [jnp.zeros((SUB, LANE), F32) for _ in heads]
        for s in range(SUB):
            for h, hs in enumerate(heads):
                c_s = c_ref[h, pl.ds(SUB * i + s, SUB, stride=0), :]
                w = q_r[:, hs] * jnp.exp2(b_r[:, hs] - c_s)
                col = jnp.sum(w, axis=-1, keepdims=True)
                a_i[h] = jnp.where(lane == SUB * i + s, col, a_i[h])
        for h in range(H_A):
            a_rows[h].append(a_i[h])
    v_pad = jnp.concatenate([v_bf, jnp.zeros((LANE - CHUNK, D_A), BF16)], axis=0)
    for h, hs in enumerate(heads):
        a = jnp.where(causal, jnp.concatenate(a_rows[h], axis=0), 0.0)
        o[h] = o[h] + _dot(a.astype(BF16), v_pad[:, hs])
    return o, st_new


def _forget_floor(lbl_ref):
    lbl = lbl_ref[...]
    e = jnp.exp(lbl - jnp.max(lbl, axis=0, keepdims=True))
    return e[0:1, :] / jnp.sum(e, axis=0, keepdims=True)


N_PROJ = 4


def _hgrn_project(stage, u, win_ref, lb, dst):
    q_s, k_s, g_s, v_s, gate_s = dst
    y = _dot(u, win_ref[:, stage * D_A:(stage + 1) * D_A])
    if stage == 0:
        q_s[...] = y * _sigmoid(y) * (DK_A ** -0.5)
    elif stage == 1:
        forget = lb + (1.0 - lb) * _sigmoid(y)
        k_s[...] = jnp.maximum(1.0 - forget, 0.0)
        g_s[...] = jnp.log2(forget)
    elif stage == 2:
        v_s[...] = y
    else:
        gate_s[...] = y * _sigmoid(y)


def _hgrn_run_chunk(src, rs, st_s, c_s, consts, gg, o_ref, out_rs):
    q_s, k_s, g_s, v_s, gate_s = src
    o, st_new = _hgrn_chunk(q_s[rs, :], k_s[rs, :], g_s[rs, :], v_s[rs, :],
                            [st_s[h] for h in range(H_A)], *consts, c_s)
    for h in range(H_A):
        hs = slice(h * DV_A, (h + 1) * DV_A)
        st_s[h] = st_new[h]
        o_ref[0, out_rs, hs] = (_rms(o[h]) * gg * gate_s[rs, hs]).astype(BF16)


def _hgrn_kernel(x_ref, s0_ref, gpre_ref, win_ref, lbl_ref, gg_ref,
                 o_ref, sout_ref,
                 q_s, k_s, g_s, v_s, gate_s, st_s, c_s, *, tile):
    t = pl.program_id(1)

    @pl.when(t == 0)
    def _():
        for h in range(H_A):
            st_s[h] = s0_ref[0, h].T

    lb = _forget_floor(lbl_ref)
    rows = (q_s, k_s, g_s, v_s, gate_s)
    u = (_rms(x_ref[0]) * gpre_ref[...]).astype(BF16)
    for stage in range(N_PROJ):
        _hgrn_project(stage, u, win_ref, lb, rows)

    gg = gg_ref[...]
    consts = _hgrn_consts()

    def chunk_body(c, carry):
        rs = pl.ds(pl.multiple_of(c * CHUNK, CHUNK), CHUNK)
        _hgrn_run_chunk(rows, rs, st_s, c_s, consts, gg, o_ref, rs)
        return carry

    lax.fori_loop(0, tile // CHUNK, chunk_body, 0)

    @pl.when(t == pl.num_programs(1) - 1)
    def _():
        for h in range(H_A):
            sout_ref[0, h] = st_s[h].T


def _hgrn_pipe_kernel(x_ref, xn_ref, s0_ref, gpre_ref, win_ref, lbl_ref, gg_ref,
                      o_ref, sout_ref, *scratch, tile):
    set_a, set_b = scratch[0:5], scratch[5:10]
    st_s, c_s = scratch[10:12]
    t = pl.program_id(1)
    lb = _forget_floor(lbl_ref)
    gpre = gpre_ref[...]
    gg = gg_ref[...]
    consts = _hgrn_consts()
    norm_in = lambda x: (_rms(x) * gpre).astype(BF16)

    @pl.when(t == 0)
    def _():
        for h in range(H_A):
            st_s[h] = s0_ref[0, h].T
        u0 = norm_in(x_ref[0, 0:tile, :])
        for stage in range(N_PROJ):
            _hgrn_project(stage, u0, win_ref, lb, set_a)

    def phase(src, x_next, dst, out_row0):
        u = norm_in(x_next)
        for c in range(tile // CHUNK):
            rs = slice(c * CHUNK, (c + 1) * CHUNK)
            out_rs = slice(out_row0 + c * CHUNK, out_row0 + (c + 1) * CHUNK)
            _hgrn_run_chunk(src, rs, st_s, c_s, consts, gg, o_ref, out_rs)
            _hgrn_project(c, u, win_ref, lb, dst)

    phase(set_a, x_ref[0, tile:2 * tile, :], set_b, 0)
    phase(set_b, xn_ref[0], set_a, tile)

    @pl.when(t == pl.num_programs(1) - 1)
    def _():
        for h in range(H_A):
            sout_ref[0, h] = st_s[h].T


def _hgrn_call(x, s0, gpre, w_in, lb_logits, ggain, *, tile):
    B, L, _ = x.shape
    pipelined = L % (2 * tile) == 0 and tile // CHUNK == N_PROJ
    row_scratch = pltpu.VMEM((tile, D_A), F32)
    state_spec = pl.BlockSpec((1, H_A, DK_A, DV_A), lambda b, t: (b, 0, 0, 0))
    consts = [_const_spec((1, D_MODEL)), _const_spec((D_MODEL, 4 * D_A)),
              _const_spec(lb_logits.shape), _const_spec((1, DV_A))]
    if pipelined:
        step = 2 * tile
        last_tile = L // tile - 1
        kern = functools.partial(_hgrn_pipe_kernel, tile=tile)
        x_specs = [pl.BlockSpec((1, step, D_MODEL), lambda b, t: (b, t, 0)),
                   pl.BlockSpec((1, tile, D_MODEL),
                                lambda b, t: (b, jnp.minimum(2 * t + 2, last_tile), 0))]
        x_args = (x, x)
        n_sets = 2
    else:
        step = tile
        kern = functools.partial(_hgrn_kernel, tile=tile)
        x_specs = [pl.BlockSpec((1, tile, D_MODEL), lambda b, t: (b, t, 0))]
        x_args = (x,)
        n_sets = 1
    return pl.pallas_call(
        kern,
        grid=(B, L // step),
        in_specs=x_specs + [state_spec] + consts,
        out_specs=[pl.BlockSpec((1, step, D_A), lambda b, t: (b, t, 0)), state_spec],
        out_shape=[
            jax.ShapeDtypeStruct((B, L, D_A), BF16),
            jax.ShapeDtypeStruct((B, H_A, DK_A, DV_A), F32),
        ],
        scratch_shapes=[row_scratch] * (5 * n_sets) + [pltpu.VMEM((H_A, DV_A, DK_A), F32),
                                                       pltpu.VMEM((H_A, CHUNK, DK_A), F32)],
        compiler_params=pltpu.CompilerParams(
            dimension_semantics=("arbitrary", "arbitrary"),
            vmem_limit_bytes=VMEM_LIMIT),
        name="hgrn",
    )(*x_args, s0, gpre, w_in, lb_logits, ggain)


def _mlp_kernel(x_ref, a_ref, wo_ref, gpost_ref, gpre_ref, wup_ref, wdn_ref, gmpost_ref, y_ref):
    h = x_ref[...] + _rms(_dot(a_ref[...], wo_ref[...])) * gpost_ref[...]
    hn = (_rms(h) * gpre_ref[...]).astype(BF16)
    acc = jnp.zeros(h.shape, F32)
    for c in range(D_FF // FF_BLOCK):
        sl = slice(c * FF_BLOCK, (c + 1) * FF_BLOCK)
        u = jnp.maximum(_dot(hn, wup_ref[:, sl]), 0.0)
        acc = acc + _dot((u * u).astype(BF16), wdn_ref[sl, :])
    y_ref[...] = h + _rms(acc) * gmpost_ref[...]


def _mlp_call(x, a, w_o, gpost, gpre, w_up, w_down, gmpost, *, tile):
    n = x.shape[0]
    row = lambda i: (i, 0)
    return pl.pallas_call(
        _mlp_kernel,
        grid=(n // tile,),
        in_specs=[
            pl.BlockSpec((tile, D_MODEL), row),
            pl.BlockSpec((tile, D_MODEL), row),
            _const_spec((D_MODEL, D_MODEL)),
            _const_spec((1, D_MODEL)),
            _const_spec((1, D_MODEL)),
            _const_spec((D_MODEL, D_FF)),
            _const_spec((D_FF, D_MODEL)),
            _const_spec((1, D_MODEL)),
        ],
        out_specs=pl.BlockSpec((tile, D_MODEL), row),
        out_shape=jax.ShapeDtypeStruct((n, D_MODEL), F32),
        compiler_params=pltpu.CompilerParams(
            dimension_semantics=("arbitrary",),
            vmem_limit_bytes=VMEM_LIMIT),
        name="mlp",
    )(x, a, w_o, gpost, gpre, w_up, w_down, gmpost)


def _rope_tables(pos):
    lane = jnp.arange(LANE, dtype=jnp.int32) % DH_B
    pair = (2 * (lane % ROT_HALF)).astype(F32)
    inv_freq = jnp.where(lane < ROT_DIM, ROPE_THETA ** (-pair / ROT_DIM), 0.0)
    ang = pos.astype(F32)[:, None] * inv_freq[None, :]
    cos, sin = jnp.cos(ang), jnp.sin(ang)
    up = jnp.where(lane < ROT_HALF, -sin, 0.0)
    dn = jnp.where((lane >= ROT_HALF) & (lane < ROT_DIM), sin, 0.0)
    return cos, up, dn


def _qkv_kernel(h_ref, gkv_ref, gq_ref, wkv_ref, wq_ref, cos_ref, up_ref, dn_ref,
                k_ref, v_ref, q_ref, *resident):
    tile = h_ref.shape[0]
    xh = _rms(h_ref[...])
    kv = _dot((xh * gkv_ref[...]).astype(BF16), wkv_ref[...])
    q = _dot((xh * gq_ref[...]).astype(BF16), wq_ref[...])
    cos, up, dn = cos_ref[...], up_ref[...], dn_ref[...]

    def rope(x):
        return (x * cos + pltpu.roll(x, LANE - ROT_HALF, axis=1) * up
                + pltpu.roll(x, ROT_HALF, axis=1) * dn)

    n_k = N_KV_B * HEAD_W
    for j in range(N_KV_B):
        kj = rope(kv[:, j * HEAD_W:(j + 1) * HEAD_W])
        vj = kv[:, n_k + j * HEAD_W:n_k + (j + 1) * HEAD_W]
        k_ref[pl.ds(j, tile, stride=N_KV_B), :] = kj
        v_ref[pl.ds(j, tile, stride=N_KV_B), :] = vj
        if resident:
            kb_ref, vt_ref = resident
            kb_ref[0, j] = kj.astype(BF16)
            vt_ref[0, j] = vj.T.astype(BF16)
    for j in range(H_B):
        sl = slice(j * HEAD_W, (j + 1) * HEAD_W)
        q_ref[:, sl] = (rope(q[:, sl]) * (DH_B ** -0.5 * LOG2_E)).astype(BF16)


def _qkv_call(h, gkv, gq, w_kv, w_q, tables, *, batch, tile, resident):
    n = h.shape[0]
    seq = n // batch
    per_seq = max(1, seq // tile)
    n_tab = tables[0].shape[0] // tile
    row = lambda i: (i, 0)
    tab = pl.BlockSpec((tile, LANE), lambda i: (i % n_tab, 0))
    n_k = N_KV_B * HEAD_W
    assert seq % tile == 0 or not resident
    kv_spec = pl.BlockSpec((tile * N_KV_B, HEAD_W), row)
    out_specs = [kv_spec, kv_spec, pl.BlockSpec((tile, H_B * HEAD_W), row)]
    out_shape = [
        jax.ShapeDtypeStruct((n * N_KV_B, HEAD_W), F32),
        jax.ShapeDtypeStruct((n * N_KV_B, HEAD_W), F32),
        jax.ShapeDtypeStruct((n, H_B * HEAD_W), BF16),
    ]
    if resident:
        out_specs += [
            pl.BlockSpec((1, N_KV_B, tile, HEAD_W), lambda i: (i // per_seq, 0, i % per_seq, 0)),
            pl.BlockSpec((1, N_KV_B, HEAD_W, tile), lambda i: (i // per_seq, 0, 0, i % per_seq)),
        ]
        out_shape += [
            jax.ShapeDtypeStruct((batch, N_KV_B, seq, HEAD_W), BF16),
            jax.ShapeDtypeStruct((batch, N_KV_B, HEAD_W, seq), BF16),
        ]
    return pl.pallas_call(
        _qkv_kernel,
        grid=(n // tile,),
        in_specs=[
            pl.BlockSpec((tile, D_MODEL), row),
            _const_spec((1, D_MODEL)),
            _const_spec((1, D_MODEL)),
            _const_spec((D_MODEL, 2 * n_k)),
            _const_spec((D_MODEL, H_B * HEAD_W)),
            tab, tab, tab,
        ],
        out_specs=out_specs,
        out_shape=out_shape,
        compiler_params=pltpu.CompilerParams(
            dimension_semantics=("arbitrary",),
            vmem_limit_bytes=VMEM_LIMIT),
        name="qkv",
    )(h, gkv, gq, w_kv, w_q, *tables)


N_STACK = 2 * G_B
COL_GROUP = 256
PV_LAG = 1


def _stacked_q(q, tq):
    lane = lax.broadcasted_iota(jnp.int32, (tq, HEAD_W), 1)
    zero = jnp.zeros((tq, HEAD_W), BF16)
    parts = []
    for g in range(G_B):
        qg = q[:, g * HEAD_W:(g + 1) * HEAD_W]
        parts.append(jnp.where(lane < DH_B, qg, zero))
        parts.append(jnp.where(lane >= DH_B, qg, zero))
    return jnp.concatenate(parts, axis=0)


def _stacked_q_t(q, tq):
    lane = lax.broadcasted_iota(jnp.int32, (tq, HEAD_W), 1)
    parts = []
    for g in range(G_B):
        qg = q[:, g * HEAD_W:(g + 1) * HEAD_W].astype(F32)
        parts.append(jnp.where(lane < DH_B, qg, 0.0))
        parts.append(jnp.where(lane >= DH_B, qg, 0.0))
    return jnp.concatenate(parts, axis=0).T.astype(BF16)


def _attn_consume(s, vt_blk, acc, m_prev, l_prev, mask):
    if mask is not None:
        s = jnp.where(mask, s, NEG_BIG)
    m_new = jnp.maximum(m_prev, jnp.max(s, axis=0, keepdims=True))
    alpha = jnp.exp2(m_prev - m_new)
    p = jnp.exp2(s - m_new)
    l_new = alpha * l_prev + jnp.sum(p, axis=0, keepdims=True)
    p = p.astype(BF16)
    n = s.shape[1]
    for c in range(n // COL_GROUP):
        cs = slice(c * COL_GROUP, (c + 1) * COL_GROUP)
        acc[:, cs] = acc[:, cs] * alpha[:, cs] + _dot(vt_blk, p[:, cs])
    return m_new, l_new


def _attn_consume_produce(src, dst, k_next, qs, vt_blk, acc, m_prev, l_prev, mask):
    n_grp = src.shape[0]
    cs = [slice(c * COL_GROUP, (c + 1) * COL_GROUP) for c in range(n_grp)]
    m_out, l_out, p, alpha = [], [], [], []
    for c in range(n_grp + PV_LAG):
        if c < n_grp:
            s = src[c]
            if mask is not None:
                s = jnp.where(mask[:, cs[c]], s, NEG_BIG)
            m_new = jnp.maximum(m_prev[:, cs[c]], jnp.max(s, axis=0, keepdims=True))
            alpha.append(jnp.exp2(m_prev[:, cs[c]] - m_new))
            e = jnp.exp2(s - m_new)
            l_out.append(alpha[c] * l_prev[:, cs[c]] + jnp.sum(e, axis=0, keepdims=True))
            m_out.append(m_new)
            p.append(e.astype(BF16))
            if k_next is not None:
                dst[c] = _dot_nt(k_next, qs[cs[c], :])
        d = c - PV_LAG
        if d >= 0:
            acc[d] = acc[d] * alpha[d] + _dot(vt_blk, p[d])
    return jnp.concatenate(m_out, axis=1), jnp.concatenate(l_out, axis=1)


def _chunk_mask(q_pos0, k_pos0, tq, tk):
    c = lax.broadcasted_iota(jnp.int32, (1, N_STACK * tq), 1)
    qc = (q_pos0 + c % tq) // CHUNK
    kc = (k_pos0 + lax.broadcasted_iota(jnp.int32, (tk, 1), 0)) // CHUNK
    return kc <= qc


def _lambda(lq1, lk1, lq2, lk2, lam_init):
    return (jnp.exp(jnp.sum(lq1[...] * lk1[...], axis=1, keepdims=True))
            - jnp.exp(jnp.sum(lq2[...] * lk2[...], axis=1, keepdims=True)) + lam_init)


def _attn_finish(store, acc, l, lam, sub, tq, lam_init):
    o = (acc * (1.0 / l)).T
    for g in range(G_B):
        og = o[(2 * g) * tq:(2 * g + 1) * tq] - lam * o[(2 * g + 1) * tq:(2 * g + 2) * tq]
        store(g, (_rms(og) * sub * (1.0 - lam_init)).astype(BF16))


def _attn_prompt_kernel(q_ref, kb_ref, vt_ref, lq1, lk1, lq2, lk2, sub_ref, o_ref,
                        qs, acc, s_a, s_b, ml_s, *, tq, tk, lam_init):
    i = pl.program_id(2)
    n = N_STACK * tq
    qs[...] = _stacked_q(q_ref[0], tq)
    acc[...] = jnp.zeros(acc.shape, F32)

    n_full = (i * tq + CHUNK) // tk

    def kblk(j):
        return kb_ref[0, 0, pl.ds(pl.multiple_of(j * tk, tk), tk), :]

    def vblk(j):
        return vt_ref[0, 0, :, pl.ds(pl.multiple_of(j * tk, tk), tk)]

    def pair(jj, carry):
        j = 2 * jj
        carry = _attn_consume_produce(s_a, s_b, kblk(j + 1), qs, vblk(j), acc, *carry, None)
        return _attn_consume_produce(s_b, s_a, kblk(j + 2), qs, vblk(j + 1), acc, *carry, None)

    for c in range(n // COL_GROUP):
        s_a[c] = _dot_nt(kblk(0), qs[c * COL_GROUP:(c + 1) * COL_GROUP, :])
    carry = (jnp.full((1, n), NEG_BIG, F32), jnp.zeros((1, n), F32))
    carry = lax.fori_loop(0, n_full // 2, pair, carry)
    j_tail = 2 * (n_full // 2)

    def last(buf, j, carry):
        mask = _chunk_mask(i * tq, j * tk, tq, tk)
        m, l = _attn_consume_produce(buf, None, None, qs, vblk(j), acc, *carry, mask)
        ml_s[0:1, :] = m
        ml_s[1:2, :] = l

    @pl.when(n_full % 2 == 1)
    def _():
        c2 = _attn_consume_produce(s_a, s_b, kblk(j_tail + 1), qs, vblk(j_tail), acc,
                                   *carry, None)
        last(s_b, j_tail + 1, c2)

    @pl.when(n_full % 2 == 0)
    def _():
        last(s_a, j_tail, carry)

    l = ml_s[1:2, :]

    def store(g, og):
        o_ref[0, :, g * HEAD_W:(g + 1) * HEAD_W] = og

    acc_t = jnp.concatenate([acc[c] for c in range(n // COL_GROUP)], axis=1)
    _attn_finish(store, acc_t, l, _lambda(lq1, lk1, lq2, lk2, lam_init), sub_ref[...],
                 tq, lam_init)


def _attn_prompt_call(q, kb, vt, lam_params, sub_gain, *, tq, tk, lam_init):
    B, L, _ = q.shape
    assert L % tq == 0 and L % tk == 0 and tq % CHUNK == 0
    assert all(((i + 1) * tq + tk - 1) // tk - (i * tq + CHUNK) // tk == 1 for i in range(L // tq))
    n = N_STACK * tq
    q_spec = pl.BlockSpec((1, tq, G_B * HEAD_W), lambda b, h, i: (b, i, h))
    small = [_const_spec((1, DH_B))] * 4 + [_const_spec((1, HEAD_W))]
    return pl.pallas_call(
        functools.partial(_attn_prompt_kernel, tq=tq, tk=tk, lam_init=lam_init),
        grid=(B, N_KV_B, L // tq),
        in_specs=[
            q_spec,
            pl.BlockSpec((1, 1, L, HEAD_W), lambda b, h, i: (b, h, 0, 0)),
            pl.BlockSpec((1, 1, HEAD_W, L), lambda b, h, i: (b, h, 0, 0)),
        ] + small,
        out_specs=q_spec,
        out_shape=jax.ShapeDtypeStruct((B, L, H_B * HEAD_W), BF16),
        scratch_shapes=[pltpu.VMEM((n, HEAD_W), BF16),
                        pltpu.VMEM((n // COL_GROUP, HEAD_W, COL_GROUP), F32),
                        pltpu.VMEM((n // COL_GROUP, tk, COL_GROUP), F32),
                        pltpu.VMEM((n // COL_GROUP, tk, COL_GROUP), F32),
                        pltpu.VMEM((2, n), F32)],
        compiler_params=pltpu.CompilerParams(
            dimension_semantics=("arbitrary",) * 3,
            vmem_limit_bytes=VMEM_LIMIT),
        name="attn_prompt",
    )(q, kb, vt, *lam_params, sub_gain)


def _attn_sample_kernel(q_ref, pk_ref, pv_ref, nk_ref, nv_ref, lq1, lk1, lq2, lk2, sub_ref, o_ref,
                        qst, acc, m_s, l_s, *, tq, n_past, past_len, lam_init):
    j = pl.program_id(1)

    @pl.when(j == 0)
    def _():
        q = q_ref[0]
        for h in range(N_KV_B):
            qst[h] = _stacked_q_t(q[:, h * G_B * HEAD_W:(h + 1) * G_B * HEAD_W], tq)
        acc[...] = jnp.zeros(acc.shape, F32)
        m_s[...] = jnp.full(m_s.shape, NEG_BIG, F32)
        l_s[...] = jnp.zeros(l_s.shape, F32)

    def run(k_ref, v_ref, mask):
        keys = k_ref.shape[1] // N_KV_B
        head = lambda ref, h: ref[0, pl.ds(h, keys, stride=N_KV_B), :]
        ss = [_dot(head(k_ref, h).astype(BF16), qst[h]) for h in range(N_KV_B)]
        vts = [head(v_ref, h).T.astype(BF16) for h in range(N_KV_B)]
        for h in range(N_KV_B):
            m_new, l_new = _attn_consume(ss[h], vts[h], acc.at[h], m_s[h], l_s[h], mask)
            m_s[h] = m_new
            l_s[h] = l_new

    @pl.when(j < n_past)
    def _():
        run(pk_ref, pv_ref, None)

    @pl.when(j == n_past)
    def _():
        run(nk_ref, nv_ref, _chunk_mask(past_len, past_len, tq, tq))
        lam = _lambda(lq1, lk1, lq2, lk2, lam_init)
        for h in range(N_KV_B):
            def store(g, og, h=h):
                c0 = (h * G_B + g) * HEAD_W
                o_ref[0, :, c0:c0 + HEAD_W] = og
            _attn_finish(store, acc[h], l_s[h], lam, sub_ref[...], tq, lam_init)


def _attn_sample_call(q, past_k, past_v, k_new, v_new, lam_params, sub_gain, *, tk, lam_init):
    B, tq, _ = q.shape
    past_len = past_k.shape[1]
    assert past_len % tk == 0 and past_len % CHUNK == 0 and tq <= CHUNK
    n_past = past_len // tk
    n = N_STACK * tq
    rows = lambda a: a.reshape(B, -1, HEAD_W)
    past_k, past_v, k_new, v_new = rows(past_k), rows(past_v), rows(k_new), rows(v_new)
    past_spec = pl.BlockSpec((1, tk * N_KV_B, HEAD_W), lambda b, j: (b, jnp.minimum(j, n_past - 1), 0))
    new_spec = pl.BlockSpec((1, tq * N_KV_B, HEAD_W), lambda b, j: (b, 0, 0))
    q_spec = pl.BlockSpec((1, tq, H_B * HEAD_W), lambda b, j: (b, 0, 0))
    small = [_const_spec((1, DH_B))] * 4 + [_const_spec((1, HEAD_W))]
    return pl.pallas_call(
        functools.partial(_attn_sample_kernel, tq=tq, n_past=n_past,
                          past_len=past_len, lam_init=lam_init),
        grid=(B, n_past + 1),
        in_specs=[q_spec, past_spec, past_spec, new_spec, new_spec] + small,
        out_specs=q_spec,
        out_shape=jax.ShapeDtypeStruct((B, tq, H_B * HEAD_W), BF16),
        scratch_shapes=[
            pltpu.VMEM((N_KV_B, HEAD_W, n), BF16),
            pltpu.VMEM((N_KV_B, HEAD_W, n), F32),
            pltpu.VMEM((N_KV_B, 1, n), F32),
            pltpu.VMEM((N_KV_B, 1, n), F32),
        ],
        compiler_params=pltpu.CompilerParams(
            dimension_semantics=("arbitrary",) * 2,
            vmem_limit_bytes=VMEM_LIMIT),
        name="attn_sample",
    )(q, past_k, past_v, k_new, v_new, *lam_params, sub_gain)


def _pick_tile(n, want):
    t = min(n, want)
    assert n % t == 0
    return t


def kernel(x_prompt, x_sample, state_hgrn, cache_k, cache_v, norm_mix_pre, norm_mix_post,
           norm_mlp_pre, norm_mlp_post, w_up, w_down, w_in_a, lb_logits, gnorm_a, w_out_a,
           norm_kv, w_kv, w_q_b, lam_q1, lam_k1, lam_q2, lam_k2, subln_b, w_out_b):
    row = lambda a: a.reshape(1, -1)
    w_in_bf = w_in_a[0].astype(BF16)
    w_out_a_bf = w_out_a[0].astype(BF16)
    w_up_bf = w_up.astype(BF16)
    w_down_bf = w_down.astype(BF16)
    w_kv_bf = w_kv.astype(BF16)
    w_q_bf = w_q_b[0].astype(BF16)
    w_out_b_bf = w_out_b[0].astype(BF16)
    lam_params = [row(lam_q1[0]), row(lam_k1[0]), row(lam_q2[0]), row(lam_k2[0])]
    lam_init = 0.8 - 0.6 * math.exp(-LAMBDA_INIT_SCALE * 1)

    def mlp(layer, x, a, w_o):
        return _mlp_call(x, a, w_o, row(norm_mix_post[layer]), row(norm_mlp_pre[layer]),
                         w_up_bf[layer], w_down_bf[layer], row(norm_mlp_post[layer]),
                         tile=_pick_tile(x.shape[0], 512))

    def trunk(x, s0, past_k, past_v):
        B, L, _ = x.shape
        n = B * L
        o_a, s_out = _hgrn_call(x, s0, row(norm_mix_pre[0]), w_in_bf, lb_logits, row(gnorm_a[0]),
                                tile=_pick_tile(L, 256))
        h = mlp(0, x.reshape(n, D_MODEL), o_a.reshape(n, D_A), w_out_a_bf)

        pos0 = 0 if past_k is None else past_k.shape[1]
        tile = _pick_tile(n, 512)
        reps = max(1, tile // L)
        tables = _rope_tables(pos0 + jnp.arange(L, dtype=jnp.int32))
        tables = [jnp.tile(t, (reps, 1)) for t in tables]
        outs = _qkv_call(h, row(norm_kv), row(norm_mix_pre[1]), w_kv_bf, w_q_bf, tables,
                         batch=B, tile=tile, resident=past_k is None)
        k, v, q = outs[:3]
        k = k.reshape(B, L, N_KV_B, HEAD_W)
        v = v.reshape(B, L, N_KV_B, HEAD_W)
        q3 = q.reshape(B, L, H_B * HEAD_W)
        if past_k is None:
            o_b = _attn_prompt_call(q3, outs[3], outs[4], lam_params, row(subln_b[0]),
                                    tq=512, tk=512, lam_init=lam_init)
        else:
            o_b = _attn_sample_call(q3, past_k, past_v, k, v, lam_params, row(subln_b[0]),
                                    tk=1024, lam_init=lam_init)
        y = mlp(1, h, o_b.reshape(n, D_MODEL), w_out_b_bf)
        return y.reshape(B, L, D_MODEL), s_out[None], k, v

    s0_prompt = jnp.zeros((x_prompt.shape[0], H_A, DK_A, DV_A), F32)
    y_p, s_p, k_p, v_p = trunk(x_prompt, s0_prompt, None, None)
    y_s, s_s, k_s, v_s = trunk(x_sample, state_hgrn[0], cache_k, cache_v)
    return (y_p, y_s, s_p, k_p, v_p, s_s, k_s, v_s)
```
